```python
import math
import jax, jax.numpy as jnp
from jax import lax
import numpy as np

D_MODEL = 2048
BATCH = 4
SEQ = 2048
DEPTH = 1
DEC_BATCH = 128
DEC_SEQ = 1
PAST_LEN = 16384
PAGE_SIZE = 128

PLE_DIM = 256
MIX_WIDTH = D_MODEL
A_HEADS = 8
A_NOPE = 128
A_ROPE = 64
A_V = 128
A_WIDTH = A_HEADS * A_V
Q_LORA = 512
KV_LORA = 256
B_HEADS = 8
B_KV_HEADS = 2
B_HD = 128
B_GROUP = B_HEADS // B_KV_HEADS
B_WIDTH = B_HEADS * B_HD
IDX_HEADS = 16
IDX_DIM = 64
IDX_ROPE = 32
IDX_TOPK = 256
REL_BUCKETS = 32
REL_MAX_DIST = 128
ROPE_THETA = 10000.0
Q_BLOCK = 128
EPS = 1e-6
MLA_SCALE = (A_NOPE + A_ROPE) ** -0.5
B_SCALE = B_HD ** -0.5
IDX_W_SCALE = (IDX_HEADS ** -0.5) * (IDX_DIM ** -0.5)
IN_SIZES = (Q_LORA, KV_LORA, A_ROPE, A_WIDTH,
            B_WIDTH, B_KV_HEADS * B_HD, B_KV_HEADS * B_HD, B_WIDTH,
            IDX_HEADS * IDX_DIM, IDX_DIM, IDX_HEADS)
IN_WIDTH = sum(IN_SIZES)
IN_OFFSETS = tuple(int(o) for o in np.cumsum(IN_SIZES)[:-1])

kernel_name = 'hymba_mla_dsa_decoder_step'


def rmsnorm(x, g):
    xf = x.astype(jnp.float32)
    y = xf * lax.rsqrt(jnp.mean(xf * xf, axis=-1, keepdims=True) + EPS)
    return (y * g.astype(jnp.float32)).astype(x.dtype)


def rope(x, pos):
    half = x.shape[-1] // 2
    inv = ROPE_THETA ** (-jnp.arange(half, dtype=jnp.float32) / half)
    ang = pos.astype(jnp.float32)[:, None] * inv[None, :]
    ang = ang.reshape((pos.shape[0],) + (1,) * (x.ndim - 3) + (half,))
    cos = jnp.cos(ang).astype(x.dtype)
    sin = jnp.sin(ang).astype(x.dtype)
    x1, x2 = x[..., :half], x[..., half:]
    return jnp.concatenate([x1 * cos - x2 * sin, x2 * cos + x1 * sin], axis=-1)


def rope_partial(x, pos, n_rot):
    return jnp.concatenate([rope(x[..., :n_rot], pos), x[..., n_rot:]], axis=-1)


def t5_bucket(rel):
    n = jnp.maximum(rel, 0)
    exact = REL_BUCKETS // 2
    nf = jnp.maximum(n, 1).astype(jnp.float32)
    large = exact + (jnp.log(nf / exact) / math.log(REL_MAX_DIST / exact)
                     * (REL_BUCKETS - exact)).astype(jnp.int32)
    large = jnp.minimum(large, REL_BUCKETS - 1)
    return jnp.where(n < exact, n, large)


def branch_inputs(x, pos, g_norm, w_in, g_q, w_uq, g_kv, w_uk):
    nb, nt = x.shape[0], x.shape[1]
    h = rmsnorm(x, g_norm)
    z = h @ w_in
    (c_q, c_kv, k_pe, gate_a, q_b, k_b, v_b, gate_b,
     q_idx, k_idx, w_idx) = jnp.split(z, IN_OFFSETS, axis=-1)
    q = (rmsnorm(c_q, g_q) @ w_uq).reshape(nb, nt, A_HEADS, A_NOPE + A_ROPE)
    q_lat = jnp.einsum('bthn,hnr->bthr', q[..., :A_NOPE], w_uk)
    q_pe = rope(q[..., A_NOPE:], pos)
    c_kv = rmsnorm(c_kv, g_kv)
    k_pe = rope(k_pe, pos)
    q_b = q_b.reshape(nb, nt, B_KV_HEADS, B_GROUP, B_HD)
    k_b = k_b.reshape(nb, nt, B_KV_HEADS, B_HD)
    v_b = v_b.reshape(nb, nt, B_KV_HEADS, B_HD)
    q_idx = rope_partial(q_idx.reshape(nb, nt, IDX_HEADS, IDX_DIM), pos, IDX_ROPE)
    k_idx = rope_partial(k_idx, pos, IDX_ROPE)
    w_idx = w_idx * IDX_W_SCALE
    return (q_lat, q_pe, c_kv, k_pe, gate_a, q_b, k_b, v_b, gate_b, q_idx, k_idx, w_idx)


def mla_attention(q_lat, q_pe, ckv, kpe, q_pos, k_pos):
    s = (jnp.einsum('bthr,blr->bhtl', q_lat, ckv)
         + jnp.einsum('bthp,blp->bhtl', q_pe, kpe)).astype(jnp.float32) * MLA_SCALE
    mask = k_pos[None, :] <= q_pos[:, None]
    p = jax.nn.softmax(jnp.where(mask, s, -jnp.inf), axis=-1).astype(ckv.dtype)
    return jnp.einsum('bhtl,blr->bthr', p, ckv)


def dsa_select(qi, w, kidx, q_pos, k_pos, k_top):
    s = jax.nn.relu(jnp.einsum('bthd,bld->bthl', qi, kidx))
    score = jnp.einsum('bthl,bth->btl', s, w).astype(jnp.float32)
    mask = (k_pos[None, :] <= q_pos[:, None])[None]
    _, idx = lax.top_k(jnp.where(mask, score, -jnp.inf), k_top)
    return idx


def dsa_attend(q, k_sel, v_sel, q_pos, sel_pos, rel_table):
    nb, nt, nk = sel_pos.shape
    s = jnp.einsum('btngd,btknd->btngk', q, k_sel).astype(jnp.float32) * B_SCALE
    bias = rel_table[t5_bucket(q_pos[None, :, None] - sel_pos)]
    bias = bias.reshape(nb, nt, nk, B_KV_HEADS, B_GROUP).transpose(0, 1, 3, 4, 2)
    s = s + bias.astype(jnp.float32)
    valid = (sel_pos <= q_pos[None, :, None])[:, :, None, None, :]
    p = jax.nn.softmax(jnp.where(valid, s, -jnp.inf), axis=-1).astype(v_sel.dtype)
    o = jnp.einsum('btngk,btknd->btngd', p, v_sel)
    return o.reshape(nb, nt, B_WIDTH)


def gather_paged(pool, layer, row, new_rows, idx):
    past = row.shape[0] * PAGE_SIZE
    is_past = idx < past
    pi = jnp.minimum(idx, past - 1)
    rows_past = pool[layer, row[pi // PAGE_SIZE], pi % PAGE_SIZE]
    rows_new = new_rows[jnp.clip(idx - past, 0, new_rows.shape[0] - 1)]
    mask = is_past.reshape(idx.shape + (1,) * (rows_past.ndim - idx.ndim))
    return jnp.where(mask, rows_past, rows_new)


def prompt_mix(q_lat, q_pe, c_kv, k_pe, q_b, k_b, v_b, q_idx, k_idx, w_idx, rel_table):
    nb, seq = q_lat.shape[0], q_lat.shape[1]
    n_blk = seq // Q_BLOCK
    k_top = min(IDX_TOPK, seq // 4)
    k_pos = jnp.arange(seq, dtype=jnp.int32)

    def blocks(a):
        return a.reshape((nb, n_blk, Q_BLOCK) + a.shape[2:]).swapaxes(0, 1)

    def unblock(a):
        return a.swapaxes(0, 1).reshape((nb, seq) + a.shape[3:])

    take = jax.vmap(lambda rows, i: rows[i])

    def step(args):
        ql, qp, qb, qi, wi, q_pos = args
        lat = mla_attention(ql, qp, c_kv, k_pe, q_pos, k_pos)
        idx = dsa_select(qi, wi, k_idx, q_pos, k_pos, k_top)
        o_b = dsa_attend(qb, take(k_b, idx), take(v_b, idx), q_pos, idx, rel_table)
        return lat, o_b

    lat, o_b = lax.map(step, (blocks(q_lat), blocks(q_pe), blocks(q_b), blocks(q_idx),
                              blocks(w_idx), k_pos.reshape(n_blk, Q_BLOCK)))
    return unblock(lat), unblock(o_b)


def sample_mix(q_lat, q_pe, c_kv, k_pe, q_b, k_b, v_b, q_idx, k_idx, w_idx, page_table,
               pool_ckv, pool_kpe, pool_k, pool_v, pool_kidx, layer, rel_table):
    nt = q_lat.shape[1]
    past = page_table.shape[1] * PAGE_SIZE
    n_keys = past + nt
    k_top = min(IDX_TOPK, n_keys // 4)
    q_pos = past + jnp.arange(nt, dtype=jnp.int32)
    k_pos = jnp.arange(n_keys, dtype=jnp.int32)

    def step(args):
        ql, qp, ckv_new, kpe_new, qb, kb_new, vb_new, qi, wi, kidx_new, row = args

        def with_past(pool, new):
            rows = pool[layer, row].reshape((past,) + pool.shape[3:])
            return jnp.concatenate([rows, new], axis=0)[None]

        lat = mla_attention(ql[None], qp[None], with_past(pool_ckv, ckv_new),
                            with_past(pool_kpe, kpe_new), q_pos, k_pos)[0]
        idx = dsa_select(qi[None], wi[None], with_past(pool_kidx, kidx_new), q_pos, k_pos, k_top)[0]
        k_sel = gather_paged(pool_k, layer, row, kb_new, idx)
        v_sel = gather_paged(pool_v, layer, row, vb_new, idx)
        o_b = dsa_attend(qb[None], k_sel[None], v_sel[None], q_pos, idx[None], rel_table)[0]
        return lat, o_b

    return lax.map(step, (q_lat, q_pe, c_kv, k_pe, q_b, k_b, v_b, q_idx, w_idx, k_idx, page_table))


def branch_output(lat, o_b, gate_a, gate_b, w_uv, w_out):
    nb, nt = lat.shape[0], lat.shape[1]
    o_a = jnp.einsum('bthr,hrv->bthv', lat, w_uv).reshape(nb, nt, A_WIDTH)
    mixed = jnp.concatenate([o_a * jax.nn.silu(gate_a), o_b * jax.nn.silu(gate_b)], axis=-1)
    return mixed @ w_out


def ple_add(x, p, g, w_gate, w_proj):
    return x + jax.nn.sigmoid(rmsnorm(x, g) @ w_gate) * (p @ w_proj)


def setup_inputs(seed: int = 0) -> dict:
    key = jax.random.key(seed)
    ks = jax.random.split(key, 24)
    n_pages = PAST_LEN // PAGE_SIZE
    n_used = DEC_BATCH * n_pages
    n_pool = n_used + n_used // 4

    def nrm(k, shape, scale=1.0):
        return jax.random.normal(k, shape, jnp.float32) * scale

    page_table = jax.random.permutation(ks[0], n_pool)[:n_used].reshape(DEC_BATCH, n_pages).astype(jnp.int32)
    return {
        'x_prompt': nrm(ks[1], (BATCH, SEQ, D_MODEL)),
        'x_sample': nrm(ks[2], (DEC_BATCH, DEC_SEQ, D_MODEL)),
        'p_prompt': nrm(ks[3], (DEPTH, BATCH, SEQ, PLE_DIM)),
        'p_sample': nrm(ks[4], (DEPTH, DEC_BATCH, DEC_SEQ, PLE_DIM)),
        'cache_ckv': nrm(ks[5], (DEPTH, n_pool, PAGE_SIZE, KV_LORA)),
        'cache_kpe': nrm(ks[6], (DEPTH, n_pool, PAGE_SIZE, A_ROPE)),
        'cache_k': nrm(ks[7], (DEPTH, n_pool, PAGE_SIZE, B_KV_HEADS, B_HD)),
        'cache_v': nrm(ks[8], (DEPTH, n_pool, PAGE_SIZE, B_KV_HEADS, B_HD)),
        'cache_kidx': nrm(ks[9], (DEPTH, n_pool, PAGE_SIZE, IDX_DIM)),
        'page_table': page_table,
        'rel_table': nrm(ks[10], (REL_BUCKETS, B_HEADS), 0.5),
        'g_attn': 1.0 + nrm(ks[11], (DEPTH, D_MODEL), 0.02),
        'w_in': nrm(ks[12], (DEPTH, D_MODEL, IN_WIDTH), D_MODEL ** -0.5),
        'g_q': 1.0 + nrm(ks[13], (DEPTH, Q_LORA), 0.02),
        'w_uq': nrm(ks[14], (DEPTH, Q_LORA, A_HEADS * (A_NOPE + A_ROPE)), Q_LORA ** -0.5),
        'g_kv': 1.0 + nrm(ks[15], (DEPTH, KV_LORA), 0.02),
        'w_uk': nrm(ks[16], (DEPTH, A_HEADS, A_NOPE, KV_LORA), KV_LORA ** -0.5),
        'w_uv': nrm(ks[17], (DEPTH, A_HEADS, KV_LORA, A_V), KV_LORA ** -0.5),
        'w_out': nrm(ks[18], (DEPTH, MIX_WIDTH, D_MODEL), MIX_WIDTH ** -0.5),
        'g_ple': 1.0 + nrm(ks[19], (DEPTH, D_MODEL), 0.02),
        'w_ple_gate': nrm(ks[20], (DEPTH, D_MODEL, D_MODEL), D_MODEL ** -0.5),
        'w_ple_proj': nrm(ks[21], (DEPTH, PLE_DIM, D_MODEL), PLE_DIM ** -0.5),
        'g_final': 1.0 + nrm(ks[22], (D_MODEL,), 0.02),
    }


def reference(x_prompt, x_sample, p_prompt, p_sample, cache_ckv, cache_kpe, cache_k, cache_v,
              cache_kidx, page_table, rel_table, g_attn, w_in, g_q, w_uq, g_kv, w_uk, w_uv,
              w_out, g_ple, w_ple_gate, w_ple_proj, g_final):
    seq = x_prompt.shape[1]
    nt = x_sample.shape[1]
    past = page_table.shape[1] * PAGE_SIZE
    pos_p = jnp.arange(seq, dtype=jnp.int32)
    pos_s = past + jnp.arange(nt, dtype=jnp.int32)
    xp, xs = x_prompt, x_sample
    ckv_p, kpe_p, k_p, v_p, kidx_p = [], [], [], [], []
    ckv_s, kpe_s, k_s, v_s, kidx_s = [], [], [], [], []
    for i in range(DEPTH):
        (ql, qp, ckv, kpe, ga, qb, kb, vb, gb, qi, ki, wi) = branch_inputs(
            xp, pos_p, g_attn[i], w_in[i], g_q[i], w_uq[i], g_kv[i], w_uk[i])
        lat, ob = prompt_mix(ql, qp, ckv, kpe, qb, kb, vb, qi, ki, wi, rel_table)
        xp = xp + branch_output(lat, ob, ga, gb, w_uv[i], w_out[i])
        xp = ple_add(xp, p_prompt[i], g_ple[i], w_ple_gate[i], w_ple_proj[i])
        ckv_p.append(ckv); kpe_p.append(kpe); k_p.append(kb); v_p.append(vb); kidx_p.append(ki)
        (ql, qp, ckv, kpe, ga, qb, kb, vb, gb, qi, ki, wi) = branch_inputs(
            xs, pos_s, g_attn[i], w_in[i], g_q[i], w_uq[i], g_kv[i], w_uk[i])
        lat, ob = sample_mix(ql, qp, ckv, kpe, qb, kb, vb, qi, ki, wi, page_table,
                             cache_ckv, cache_kpe, cache_k, cache_v, cache_kidx, i, rel_table)
        xs = xs + branch_output(lat, ob, ga, gb, w_uv[i], w_out[i])
        xs = ple_add(xs, p_sample[i], g_ple[i], w_ple_gate[i], w_ple_proj[i])
        ckv_s.append(ckv); kpe_s.append(kpe); k_s.append(kb); v_s.append(vb); kidx_s.append(ki)
    y_prompt = rmsnorm(xp, g_final)
    y_sample = rmsnorm(xs, g_final)
    return (y_prompt, y_sample,
            jnp.stack(ckv_p), jnp.stack(kpe_p), jnp.stack(k_p), jnp.stack(v_p), jnp.stack(kidx_p),
            jnp.stack(ckv_s), jnp.stack(kpe_s), jnp.stack(k_s), jnp.stack(v_s), jnp.stack(kidx_s))
```

```python
import functools
import math

import numpy as np
import jax
import jax.numpy as jnp
from jax import lax
from jax.experimental import pallas as pl
from jax.experimental.pallas import tpu as pltpu

F32 = jnp.float32
BF16 = jnp.bfloat16
I32 = jnp.int32

A_HEADS = 8
A_NOPE = 128
A_ROPE = 64
A_V = 128
A_WIDTH = A_HEADS * A_V
Q_LORA = 512
KV_LORA = 256
B_HEADS = 8
B_KV_HEADS = 2
B_HD = 128
B_GROUP = B_HEADS // B_KV_HEADS
B_WIDTH = B_HEADS * B_HD
B_KVW = B_KV_HEADS * B_HD
IDX_HEADS = 16
IDX_DIM = 64
IDX_ROPE = 32
IDX_TOPK = 256
REL_BUCKETS = 32
REL_MAX_DIST = 128
ROPE_THETA = 10000.0
EPS = 1e-6
MLA_SCALE = (A_NOPE + A_ROPE) ** -0.5
B_SCALE = B_HD ** -0.5
IDX_W_SCALE = (IDX_HEADS ** -0.5) * (IDX_DIM ** -0.5)
IN_SIZES = (Q_LORA, KV_LORA, A_ROPE, A_WIDTH, B_WIDTH, B_KVW, B_KVW, B_WIDTH,
            IDX_HEADS * IDX_DIM, IDX_DIM, IDX_HEADS)

LANES = 128
NEG = -1e30
VMEM_LIMIT = 56 * 1024 * 1024
INT_MIN = -2 ** 31

C_Q = 0
C_KV = C_Q + Q_LORA
C_GA = C_KV + KV_LORA
C_QB = C_GA + A_WIDTH
C_KB = C_QB + B_WIDTH
C_VB = C_KB + B_KVW
C_GB = C_VB + B_KVW
C_QI = C_GB + B_WIDTH
C_SLAB = C_QI + IDX_HEADS * IDX_DIM
C_WI = C_SLAB + LANES
NP_IN = C_WI + LANES
QCAT = KV_LORA + LANES
N_TAB = 11
SCORE_PAD = 8 * LANES


def _dot(a, b):
    return jnp.dot(a, b, preferred_element_type=F32)


def _dot_nt(a, b):
    return lax.dot_general(a, b, (((1,), (1,)), ((), ())), preferred_element_type=F32)


def _silu(x):
    return x / (1.0 + jnp.exp(-x))


def _params(sem=None):
    return pltpu.CompilerParams(dimension_semantics=sem, vmem_limit_bytes=VMEM_LIMIT)


def _resident(shape):
    nd = len(shape)
    return pl.BlockSpec(shape, lambda *_: (0,) * nd, pipeline_mode=pl.Buffered(1))


def _rope_tables(pos):
    def cs(half):
        inv = ROPE_THETA ** (-jnp.arange(half, dtype=F32) / half)
        ang = pos.astype(F32)[:, None] * inv[None, :]
        return jnp.cos(ang), jnp.sin(ang)
    n = pos.shape[0]
    c32, s32 = cs(A_ROPE // 2)
    c16, s16 = cs(IDX_ROPE // 2)
    z16 = jnp.zeros((n, 16), F32)
    z32 = jnp.zeros((n, 32), F32)
    z64 = jnp.zeros((n, 64), F32)
    o32 = jnp.ones((n, 32), F32)
    cat = lambda *a: jnp.concatenate(a, axis=1)
    a64, bm64, bp64 = cat(c32, c32), cat(-s32, z32), cat(z32, s32)
    a16, cm16, cp16 = cat(c16, c16, o32), cat(-s16, z16, z32), cat(z16, s16, z32)
    tabs = [cat(a64, a64), cat(bm64, bm64), cat(bp64, bp64),
            cat(a16, a16), cat(cm16, cm16), cat(cp16, cp16),
            cat(a64, a16), cat(bm64, z64), cat(bp64, z64), cat(z64, cm16), cat(z64, cp16)]
    return jnp.concatenate(tabs, axis=1)


def _proj_in_kernel(x_ref, gattn_ref, win_ref, gq_ref, wuq_ref, gkv_ref, wuk_ref, tab_ref,
                    qcat_ref, qi2_ref, qpe_ref, qidx_ref, wi_ref, qb_ref, ga_ref, gb_ref,
                    ckv_ref, kpe_ref, kb_ref, vb_ref, kidx_ref, kcat_ref, ii_ref, kbb_ref, vbb_ref,
                    h_scr):
    x = x_ref[...]
    ms = jnp.mean(x * x, axis=-1, keepdims=True)
    h_scr[...] = (x * lax.rsqrt(ms + EPS) * gattn_ref[...]).astype(BF16)

    def zcols(c0, width):
        return _dot(h_scr[...], win_ref[:, c0:c0 + width])

    def tab(k):
        return tab_ref[:, k * LANES:(k + 1) * LANES]

    tm = x.shape[0]
    lane = lax.broadcasted_iota(I32, (tm, LANES), 1)
    lo = lane < 64

    def rope(xg, a, bm, bp, shift):
        return (xg * tab(a) + pltpu.roll(xg, LANES - shift, 1) * tab(bm)
                + pltpu.roll(xg, shift, 1) * tab(bp))

    cq = zcols(C_Q, Q_LORA)
    cq = cq * lax.rsqrt(jnp.mean(cq * cq, axis=-1, keepdims=True) + EPS) * gq_ref[...]
    q = _dot(cq.astype(BF16), wuq_ref[...])
    n_nope = A_HEADS * A_NOPE
    for g in range(A_HEADS // 2):
        xg = q[:, n_nope + g * LANES:n_nope + (g + 1) * LANES]
        rg = rope(xg, 0, 1, 2, A_ROPE // 2)
        qpe_ref[:, g * LANES:(g + 1) * LANES] = rg.astype(BF16)
        qcat_ref[2 * g, :, KV_LORA:] = jnp.where(lo, rg, 0.0).astype(BF16)
        qcat_ref[2 * g + 1, :, KV_LORA:] = jnp.where(lo, 0.0, rg).astype(BF16)
    for hh in range(A_HEADS):
        qn = q[:, hh * A_NOPE:(hh + 1) * A_NOPE].astype(BF16)
        qcat_ref[hh, :, :KV_LORA] = _dot(qn, wuk_ref[hh]).astype(BF16)

    ckv = zcols(C_KV, KV_LORA)
    ckv = ckv * lax.rsqrt(jnp.mean(ckv * ckv, axis=-1, keepdims=True) + EPS) * gkv_ref[...]
    ckv_ref[...] = ckv
    kcat_ref[:, :KV_LORA] = ckv.astype(BF16)

    slab = zcols(C_SLAB, LANES)
    slab = (slab * tab(6) + pltpu.roll(slab, LANES - 32, 1) * tab(7) + pltpu.roll(slab, 32, 1) * tab(8)
            + pltpu.roll(slab, LANES - 16, 1) * tab(9) + pltpu.roll(slab, 16, 1) * tab(10))
    swapped = pltpu.roll(slab, 64, 1)
    kpe_ref[...] = slab[:, :A_ROPE]
    kidx_ref[...] = swapped[:, :IDX_DIM]
    kcat_ref[:, KV_LORA:] = jnp.where(lo, slab, swapped).astype(BF16)
    ii_ref[...] = jnp.where(lo, swapped, slab).astype(BF16)

    kb = zcols(C_KB, B_KVW)
    kb_ref[...] = kb
    kbb_ref[...] = kb.astype(BF16)
    vb = zcols(C_VB, B_KVW)
    vb_ref[...] = vb
    vbb_ref[...] = vb.astype(BF16)
    for c in range(B_WIDTH // 512):
        qb_ref[:, c * 512:(c + 1) * 512] = zcols(C_QB + c * 512, 512).astype(BF16)
        ga_ref[:, c * 512:(c + 1) * 512] = zcols(C_GA + c * 512, 512)
        gb_ref[:, c * 512:(c + 1) * 512] = zcols(C_GB + c * 512, 512)

    for c in range(IDX_HEADS * IDX_DIM // 512):
        qi = zcols(C_QI + c * 512, 512)
        for g in range(4):
            rg = rope(qi[:, g * LANES:(g + 1) * LANES], 3, 4, 5, IDX_ROPE // 2)
            col = c * 512 + g * LANES
            qidx_ref[:, col:col + LANES] = rg.astype(BF16)
            hh = col // IDX_DIM
            qi2_ref[hh] = jnp.where(lo, rg, 0.0).astype(BF16)
            qi2_ref[hh + 1] = jnp.where(lo, 0.0, rg).astype(BF16)
    wi_ref[...] = zcols(C_WI, LANES) * IDX_W_SCALE


def _proj_in(x, tabs, tab_blocks, g_attn, w_in_p, g_q, w_uq_p, g_kv, w_uk, tm):
    n, d = x.shape
    grid = (n // tm,)
    row = lambda w: pl.BlockSpec((tm, w), lambda i: (i, 0))
    head = lambda nh, w: pl.BlockSpec((nh, tm, w), lambda i: (0, i, 0))
    out_shape = [
        jax.ShapeDtypeStruct((A_HEADS, n, QCAT), BF16),
        jax.ShapeDtypeStruct((IDX_HEADS, n, LANES), BF16),
        jax.ShapeDtypeStruct((n, A_HEADS * A_ROPE), BF16),
        jax.ShapeDtypeStruct((n, IDX_HEADS * IDX_DIM), BF16),
        jax.ShapeDtypeStruct((n, LANES), F32),
        jax.ShapeDtypeStruct((n, B_WIDTH), BF16),
        jax.ShapeDtypeStruct((n, A_WIDTH), F32),
        jax.ShapeDtypeStruct((n, B_WIDTH), F32),
        jax.ShapeDtypeStruct((n, KV_LORA), F32),
        jax.ShapeDtypeStruct((n, A_ROPE), F32),
        jax.ShapeDtypeStruct((n, B_KVW), F32),
        jax.ShapeDtypeStruct((n, B_KVW), F32),
        jax.ShapeDtypeStruct((n, IDX_DIM), F32),
        jax.ShapeDtypeStruct((n, QCAT), BF16),
        jax.ShapeDtypeStruct((n, LANES), BF16),
        jax.ShapeDtypeStruct((n, B_KVW), BF16),
        jax.ShapeDtypeStruct((n, B_KVW), BF16),
    ]
    out_specs = [head(A_HEADS, QCAT), head(IDX_HEADS, LANES), row(A_HEADS * A_ROPE),
                 row(IDX_HEADS * IDX_DIM), row(LANES), row(B_WIDTH), row(A_WIDTH), row(B_WIDTH),
                 row(KV_LORA), row(A_ROPE), row(B_KVW), row(B_KVW), row(IDX_DIM),
                 row(QCAT), row(LANES), row(B_KVW), row(B_KVW)]
    in_specs = [row(d), _resident(g_attn.shape), _resident(w_in_p.shape), _resident(g_q.shape),
                _resident(w_uq_p.shape), _resident(g_kv.shape), _resident(w_uk.shape),
                pl.BlockSpec((tm, N_TAB * LANES), lambda i: (i % tab_blocks, 0))]
    return pl.pallas_call(
        _proj_in_kernel, grid=grid, in_specs=in_specs, out_specs=out_specs, out_shape=out_shape,
        scratch_shapes=[pltpu.VMEM((tm, d), BF16)],
        compiler_params=_params(("parallel",)),
    )(x, g_attn, w_in_p, g_q, w_uq_p, g_kv, w_uk, tabs)


def _flash_update(carry, s, valid, v):
    m, l, acc = carry
    if valid is not None:
        s = jnp.where(valid, s, NEG)
    m_new = jnp.maximum(m, jnp.max(s, axis=-1, keepdims=True))
    p = jnp.exp(s - m_new)
    if valid is not None:
        p = jnp.where(valid, p, 0.0)
    alpha = jnp.exp(m - m_new)
    l = alpha * l + jnp.sum(p, axis=-1, keepdims=True)
    acc = alpha * acc + _dot(p.astype(v.dtype), v)
    return m_new, l, acc


def _flash_init(m, d):
    return (jnp.full((m, 1), NEG, F32), jnp.zeros((m, 1), F32), jnp.zeros((m, d), F32))


def _mla_prompt_kernel(qcat_ref, kcat_ref, ga_ref, wuv_ref, out_ref, *, tq):
    i = pl.program_id(1)
    row = lax.broadcasted_iota(I32, (tq, tq), 0)
    col = lax.broadcasted_iota(I32, (tq, tq), 1)
    causal = col <= row
    for hh in range(A_HEADS):
        qh = qcat_ref[hh, 0]

        def chunk(j):
            kc = kcat_ref[0, pl.ds(pl.multiple_of(j * tq, tq), tq), :]
            return _dot_nt(qh, kc) * MLA_SCALE, kc[:, :KV_LORA]

        def body(j, carry):
            s, v = chunk(j)
            return _flash_update(carry, s, None, v)

        carry = lax.fori_loop(0, i, body, _flash_init(tq, KV_LORA))
        s, v = chunk(i)
        _, l, acc = _flash_update(carry, s, causal, v)
        lat = (acc / l).astype(BF16)
        o = _dot(lat, wuv_ref[hh])
        cols = slice(hh * A_V, (hh + 1) * A_V)
        out_ref[0, :, cols] = (o * _silu(ga_ref[0, :, cols])).astype(BF16)


def _mla_prompt(qcat, kcat, gate_a, w_uv, tq):
    nb, seq = kcat.shape[0], kcat.shape[1]
    return pl.pallas_call(
        functools.partial(_mla_prompt_kernel, tq=tq),
        grid=(nb, seq // tq),
        in_specs=[pl.BlockSpec((A_HEADS, 1, tq, QCAT), lambda b, i: (0, b, i, 0)),
                  pl.BlockSpec((1, seq, QCAT), lambda b, i: (b, 0, 0)),
                  pl.BlockSpec((1, tq, A_WIDTH), lambda b, i: (b, i, 0)),
                  _resident(w_uv.shape)],
        out_specs=pl.BlockSpec((1, tq, A_WIDTH), lambda b, i: (b, i, 0)),
        out_shape=jax.ShapeDtypeStruct((nb, seq, A_WIDTH), BF16),
        compiler_params=_params(("parallel", "parallel")),
    )(qcat, kcat, gate_a, w_uv)


def _sort_key(score):
    b = pltpu.bitcast(score + 0.0, I32)
    return jnp.where(b < 0, b ^ 0x7FFFFFFF, b)


def _kth_largest_key(count_ge, shape, k):
    def body(it, lo):
        bit = 31 - it
        cand = lo + lax.shift_left(jnp.int32(1), bit)
        return jnp.where(count_ge(cand) >= k, cand, lo)
    return lax.fori_loop(0, 32, body, jnp.full(shape, INT_MIN, I32))


def _dsa_prompt_kernel(cfar_ref, qi2_ref, wi_ref, ii_ref, qb_ref, kbb_ref, vbb_ref, gb_ref, bias_ref,
                       out_ref, key_scr, *, tq, seq, k_top, cw):
    i = pl.program_id(1)
    t0 = i * tq
    w = wi_ref[0]
    qrow = t0 + lax.broadcasted_iota(I32, (tq, cw), 0)
    kcol = lax.broadcasted_iota(I32, (tq, cw), 1)

    for c in range(seq // cw):
        @pl.when(c * cw < t0 + tq)
        def _():
            kc = ii_ref[0, c * cw:(c + 1) * cw, :]
            acc = jnp.zeros((tq, cw), F32)
            for hh in range(IDX_HEADS):
                s = _dot_nt(qi2_ref[hh, 0], kc)
                acc = acc + w[:, hh:hh + 1] * jnp.maximum(s, 0.0)
            acc = jnp.where(kcol + c * cw <= qrow, acc, -jnp.inf)
            key = _sort_key(acc)
            for u in range(cw // tq):
                key_scr[c * (cw // tq) + u] = key[:, u * tq:(u + 1) * tq]

        @pl.when(c * cw >= t0 + tq)
        def _():
            key = _sort_key(jnp.full((tq, tq), -jnp.inf, F32))
            for u in range(cw // tq):
                key_scr[c * (cw // tq) + u] = key

    def count_ge(cand):
        ge = (key_scr[...] >= cand[None]).astype(F32)
        return jnp.sum(jnp.sum(ge, axis=0), axis=1, keepdims=True)

    thr = _kth_largest_key(count_ge, (tq, 1), k_top)

    row = lax.broadcasted_iota(I32, (tq, tq), 0)
    col = lax.broadcasted_iota(I32, (tq, tq), 1)
    causal = col <= row
    for hh in range(B_HEADS):
        n = hh // B_GROUP
        qh = qb_ref[0, :, hh * B_HD:(hh + 1) * B_HD]
        kv_cols = slice(n * B_HD, (n + 1) * B_HD)

        def chunk(j):
            off = pl.multiple_of(j * tq, tq)
            s = _dot_nt(qh, kbb_ref[0, pl.ds(off, tq), kv_cols]) * B_SCALE
            sel = key_scr[j] >= thr
            return s, sel, vbb_ref[0, pl.ds(off, tq), kv_cols]

        def far(j, carry):
            s, sel, v = chunk(j)
            return _flash_update(carry, s + cfar_ref[hh], sel, v)

        carry = lax.fori_loop(0, jnp.maximum(i - 1, 0), far, _flash_init(tq, B_HD))

        def prev(carry):
            s, sel, v = chunk(i - 1)
            return _flash_update(carry, s + bias_ref[hh, :, :tq], sel, v)

        carry = lax.cond(i > 0, prev, lambda c: c, carry)
        s, sel, v = chunk(i)
        _, l, acc = _flash_update(carry, s + bias_ref[hh, :, tq:], sel & causal, v)
        cols = slice(hh * B_HD, (hh + 1) * B_HD)
        out_ref[0, :, cols] = ((acc / l) * _silu(gb_ref[0, :, cols])).astype(BF16)


def _dsa_prompt(cfar, qi2, wi, ii, qb, kbb, vbb, gate_b, bias, tq, k_top):
    nb, seq = ii.shape[0], ii.shape[1]
    cw = min(512, seq)
    assert cw % tq == 0 and seq % cw == 0
    full = lambda w: pl.BlockSpec((1, seq, w), lambda b, i: (b, 0, 0))
    tile = lambda w: pl.BlockSpec((1, tq, w), lambda b, i: (b, i, 0))
    return pl.pallas_call(
        functools.partial(_dsa_prompt_kernel, tq=tq, seq=seq, k_top=k_top, cw=cw),
        grid=(nb, seq // tq),
        in_specs=[pl.BlockSpec(memory_space=pltpu.SMEM),
                  pl.BlockSpec((IDX_HEADS, 1, tq, LANES), lambda b, i: (0, b, i, 0)),
                  tile(LANES), full(LANES), tile(B_WIDTH), full(B_KVW), full(B_KVW), tile(B_WIDTH),
                  _resident(bias.shape)],
        out_specs=tile(B_WIDTH),
        out_shape=jax.ShapeDtypeStruct((nb, seq, B_WIDTH), BF16),
        scratch_shapes=[pltpu.VMEM((seq // tq, tq, tq), I32)],
        compiler_params=_params(("parallel", "parallel")),
    )(cfar, qi2, wi, ii, qb, kbb, vbb, gate_b, bias)


def _bias_kernel(table_ref, bucket_ref, out_ref):
    bucket = bucket_ref[...]
    for hh in range(B_HEADS):
        acc = jnp.zeros(bucket.shape, F32)
        for b in range(REL_BUCKETS):
            acc = jnp.where(bucket == b, table_ref[b, hh], acc)
        out_ref[hh] = acc


def _t5_bucket_np(rel):
    n = np.maximum(rel, 0)
    exact = REL_BUCKETS // 2
    nf = np.maximum(n, 1).astype(np.float32)
    large = exact + (np.log(nf / exact) / math.log(REL_MAX_DIST / exact)
                     * (REL_BUCKETS - exact)).astype(np.int32)
    return np.where(n < exact, n, np.minimum(large, REL_BUCKETS - 1)).astype(np.int32)


def _bias_tiles(rel_table, rel):
    bucket = jnp.asarray(_t5_bucket_np(rel))
    return pl.pallas_call(
        _bias_kernel,
        in_specs=[pl.BlockSpec(memory_space=pltpu.SMEM), pl.BlockSpec(memory_space=pltpu.VMEM)],
        out_specs=pl.BlockSpec(memory_space=pltpu.VMEM),
        out_shape=jax.ShapeDtypeStruct((B_HEADS,) + rel.shape, F32),
    )(rel_table, bucket)


def _out_proj_kernel(x_ref, ma_ref, mb_ref, p_ref, wout_ref, gple_ref, wgate_ref, wproj_ref, gfin_ref,
                     y_ref, x1_scr, hn_scr, *, cw):
    d = x_ref.shape[1]
    nc = d // cw
    half = ma_ref.shape[1]
    ss = jnp.zeros((x_ref.shape[0], 1), F32)
    for c in range(nc):
        cols = slice(c * cw, (c + 1) * cw)
        x1 = (x_ref[:, cols] + _dot(ma_ref[...], wout_ref[:half, cols])
              + _dot(mb_ref[...], wout_ref[half:, cols]))
        x1_scr[:, cols] = x1
        ss = ss + jnp.sum(x1 * x1, axis=-1, keepdims=True)
    rs = lax.rsqrt(ss / d + EPS)
    for c in range(nc):
        cols = slice(c * cw, (c + 1) * cw)
        hn_scr[:, cols] = (x1_scr[:, cols] * rs * gple_ref[:, cols]).astype(BF16)
    pb = p_ref[...].astype(BF16)
    ss = jnp.zeros((x_ref.shape[0], 1), F32)
    for c in range(nc):
        cols = slice(c * cw, (c + 1) * cw)
        gt = 1.0 / (1.0 + jnp.exp(-_dot(hn_scr[...], wgate_ref[:, cols])))
        x2 = x1_scr[:, cols] + gt * _dot(pb, wproj_ref[:, cols])
        x1_scr[:, cols] = x2
        ss = ss + jnp.sum(x2 * x2, axis=-1, keepdims=True)
    rs = lax.rsqrt(ss / d + EPS)
    for c in range(nc):
        cols = slice(c * cw, (c + 1) * cw)
        y_ref[:, cols] = x1_scr[:, cols] * rs * gfin_ref[:, cols]


def _out_proj(x, ma, mb, p, w_out, g_ple, w_gate, w_proj, g_final, tm):
    n, d = x.shape
    row = lambda w: pl.BlockSpec((tm, w), lambda i: (i, 0))
    return pl.pallas_call(
        functools.partial(_out_proj_kernel, cw=min(512, d)),
        grid=(n // tm,),
        in_specs=[row(d), row(ma.shape[1]), row(mb.shape[1]), row(p.shape[1]),
                  _resident(w_out.shape), _resident(g_ple.shape), _resident(w_gate.shape),
                  _resident(w_proj.shape), _resident(g_final.shape)],
        out_specs=row(d),
        out_shape=jax.ShapeDtypeStruct((n, d), F32),
        scratch_shapes=[pltpu.VMEM((tm, d), F32), pltpu.VMEM((tm, d), BF16)],
        compiler_params=_params(("parallel",)),
    )(x, ma, mb, p, w_out, g_ple, w_gate, w_proj, g_final)


def _page_copies(pt_ref, pools, bufs, sems, q, chunk, slot, pages_per_chunk, page):
    copies = []
    for pool, buf, k in zip(pools, bufs, range(len(pools))):
        for p in range(pages_per_chunk):
            pid = pt_ref[q, chunk * pages_per_chunk + p]
            copies.append(pltpu.make_async_copy(
                pool.at[0, pid], buf.at[slot, pl.ds(p * page, page)], sems.at[slot, k]))
    return copies


def _sample_pass1_kernel(pt_ref, qlat_ref, qpe_ref, qidx_ref, wi_ref, ckvn_ref, kpen_ref, kidxn_ref,
                         ga_ref, wuv_ref, pool_ckv, pool_kpe, pool_kidx,
                         out_ref, score_ref,
                         buf_ckv, buf_kpe, buf_kidx, sems, m_scr, l_scr, acc_scr,
                         *, n_chunks, pages_per_chunk, page, past):
    g = pl.program_id(0)
    n_steps = pl.num_programs(0)
    q = g // n_chunks
    c = g % n_chunks
    slot = g % 2
    pools = (pool_ckv, pool_kpe, pool_kidx)
    bufs = (buf_ckv, buf_kpe, buf_kidx)
    ck = pages_per_chunk * page

    def copies(step, slot_):
        return _page_copies(pt_ref, pools, bufs, sems, step // n_chunks, step % n_chunks, slot_,
                            pages_per_chunk, page)

    @pl.when(g == 0)
    def _():
        for cp in copies(g, slot):
            cp.start()

    @pl.when(g + 1 < n_steps)
    def _():
        for cp in copies(g + 1, 1 - slot):
            cp.start()

    qlat = qlat_ref[0].astype(F32)
    qpe = qpe_ref[0].astype(F32)
    qidx = qidx_ref[0].astype(F32)
    wcol = wi_ref[0]

    @pl.when(c == 0)
    def _():
        ckvn = ckvn_ref[0]
        s_new = (jnp.sum(qlat * ckvn, axis=-1, keepdims=True)
                 + jnp.sum(qpe * kpen_ref[0], axis=-1, keepdims=True)) * MLA_SCALE
        m_scr[...] = s_new
        l_scr[...] = jnp.ones_like(s_new)
        acc_scr[...] = jnp.broadcast_to(ckvn, acc_scr.shape)
        si = jnp.maximum(jnp.sum(qidx * kidxn_ref[0], axis=-1, keepdims=True), 0.0) * wcol
        si = jnp.sum(si, axis=0, keepdims=True)
        lane = lax.broadcasted_iota(I32, (1, SCORE_PAD), 1)
        score_ref[0, :, past:] = jnp.where(lane == 0, si, -jnp.inf)

    for cp in copies(g, slot):
        cp.wait()

    kc = buf_ckv[slot]
    s = (_dot_nt(qlat, kc) + _dot_nt(qpe, buf_kpe[slot])) * MLA_SCALE
    m, l, acc = _flash_update((m_scr[...], l_scr[...], acc_scr[...]), s, None, kc)
    m_scr[...] = m
    l_scr[...] = l
    acc_scr[...] = acc

    si = jnp.maximum(_dot_nt(qidx, buf_kidx[slot]), 0.0) * wcol
    score_ref[0, :, pl.ds(pl.multiple_of(c * ck, ck), ck)] = jnp.sum(si, axis=0, keepdims=True)

    @pl.when(c == n_chunks - 1)
    def _():
        lat = (acc / l).astype(BF16)
        for hh in range(A_HEADS):
            o = _dot(lat, wuv_ref[hh])[hh:hh + 1]
            cols = slice(hh * A_V, (hh + 1) * A_V)
            out_ref[0, :, cols] = o * _silu(ga_ref[0, :, cols])


def _sample_pass1(page_table, qlat, qpe, qidx, wi, ckvn, kpen, kidxn, gate_a, w_uv,
                  pool_ckv, pool_kpe, pool_kidx, pages_per_chunk):
    nq, n_pages = page_table.shape
    page = pool_ckv.shape[2]
    past = n_pages * page
    n_chunks = n_pages // pages_per_chunk
    ck = pages_per_chunk * page
    per_q = lambda a: pl.BlockSpec((1,) + a.shape[1:], lambda g, pt: (g // n_chunks, 0, 0))
    any_spec = pl.BlockSpec(memory_space=pl.ANY)
    grid_spec = pltpu.PrefetchScalarGridSpec(
        num_scalar_prefetch=1,
        grid=(nq * n_chunks,),
        in_specs=[per_q(qlat), per_q(qpe), per_q(qidx), per_q(wi), per_q(ckvn), per_q(kpen),
                  per_q(kidxn), per_q(gate_a),
                  pl.BlockSpec(w_uv.shape, lambda g, pt: (0, 0, 0), pipeline_mode=pl.Buffered(1)),
                  any_spec, any_spec, any_spec],
        out_specs=[pl.BlockSpec((1, 1, A_WIDTH), lambda g, pt: (g // n_chunks, 0, 0)),
                   pl.BlockSpec((1, 1, past + SCORE_PAD), lambda g, pt: (g // n_chunks, 0, 0))],
        scratch_shapes=[pltpu.VMEM((2, ck, KV_LORA), F32), pltpu.VMEM((2, ck, A_ROPE), F32),
                        pltpu.VMEM((2, ck, IDX_DIM), F32), pltpu.SemaphoreType.DMA((2, 3)),
                        pltpu.VMEM((A_HEADS, 1), F32), pltpu.VMEM((A_HEADS, 1), F32),
                        pltpu.VMEM((A_HEADS, KV_LORA), F32)],
    )
    return pl.pallas_call(
        functools.partial(_sample_pass1_kernel, n_chunks=n_chunks, pages_per_chunk=pages_per_chunk,
                          page=page, past=past),
        grid_spec=grid_spec,
        out_shape=[jax.ShapeDtypeStruct((nq, 1, A_WIDTH), F32),
                   jax.ShapeDtypeStruct((nq, 1, past + SCORE_PAD), F32)],
        compiler_params=_params(("arbitrary",)),
    )(page_table, qlat, qpe, qidx, wi, ckvn, kpen, kidxn, gate_a, w_uv, pool_ckv, pool_kpe, pool_kidx)


def _topk_thresh_kernel(score_ref, thr_ref, *, k_top):
    key = _sort_key(score_ref[...])

    def count_ge(cand):
        ge = (key >= cand).astype(F32)
        return jnp.sum(jnp.sum(ge, axis=2, keepdims=True), axis=1, keepdims=True)

    thr = _kth_largest_key(count_ge, (key.shape[0], 1, 1), k_top)
    thr_ref[...] = jnp.broadcast_to(thr, thr_ref.shape)


def _topk_thresh(score3, k_top):
    nq = score3.shape[0]
    return pl.pallas_call(
        functools.partial(_topk_thresh_kernel, k_top=k_top),
        in_specs=[pl.BlockSpec(memory_space=pltpu.VMEM)],
        out_specs=pl.BlockSpec(memory_space=pltpu.VMEM),
        out_shape=jax.ShapeDtypeStruct((nq, 1, LANES), I32),
        compiler_params=_params(),
    )(score3)


def _sample_pass2_kernel(pt_ref, cfar_ref, qb_ref, kbn_ref, vbn_ref, gb_ref, score_ref, thr_ref, tail_ref,
                         pool_k, pool_v, out_ref,
                         buf_k, buf_v, sems, m_scr, l_scr, acc_scr,
                         *, n_chunks, pages_per_chunk, page, past):
    g = pl.program_id(0)
    n_steps = pl.num_programs(0)
    c = g % n_chunks
    slot = g % 2
    pools = (pool_k, pool_v)
    bufs = (buf_k, buf_v)
    ck = pages_per_chunk * page

    def copies(step, slot_):
        return _page_copies(pt_ref, pools, bufs, sems, step // n_chunks, step % n_chunks, slot_,
                            pages_per_chunk, page)

    @pl.when(g == 0)
    def _():
        for cp in copies(g, slot):
            cp.start()

    @pl.when(g + 1 < n_steps)
    def _():
        for cp in copies(g + 1, 1 - slot):
            cp.start()

    qb = qb_ref[0].astype(F32)
    thr = thr_ref[0][:, :1]
    cfar = cfar_ref[...]
    head0 = lax.broadcasted_iota(I32, (B_HEADS, 1), 0) < B_GROUP
    lo, hi = slice(0, B_HD), slice(B_HD, 2 * B_HD)

    @pl.when(c == 0)
    def _():
        key_new = _sort_key(score_ref[0, :, past:past + LANES])[:, :1]
        keep = key_new >= thr
        kbn = kbn_ref[0]
        vbn = vbn_ref[0]
        s_new = jnp.where(head0, jnp.sum(qb * kbn[:, lo], axis=-1, keepdims=True),
                          jnp.sum(qb * kbn[:, hi], axis=-1, keepdims=True))
        s_new = s_new * B_SCALE + tail_ref[:, page:page + 1]
        v_new = jnp.where(head0, jnp.broadcast_to(vbn[:, lo], (B_HEADS, B_HD)),
                          jnp.broadcast_to(vbn[:, hi], (B_HEADS, B_HD)))
        m_scr[...] = jnp.where(keep, s_new, NEG)
        l_scr[...] = jnp.where(keep, jnp.ones_like(s_new), 0.0)
        acc_scr[...] = jnp.where(keep, v_new, 0.0)

    for cp in copies(g, slot):
        cp.wait()

    off = pl.multiple_of(c * ck, ck)
    sel = _sort_key(score_ref[0, :, pl.ds(off, ck)]) >= thr
    sel = jnp.broadcast_to(sel, (B_HEADS, ck))
    lane = lax.broadcasted_iota(I32, (B_HEADS, ck), 1)
    in_tail = (lane >= ck - page) & (c == n_chunks - 1)
    if ck > page:
        tail = jnp.concatenate([jnp.zeros((B_HEADS, ck - page), F32), tail_ref[:, :page]], axis=1)
    else:
        tail = tail_ref[:, :page]
    bias = jnp.where(in_tail, tail, cfar)
    kc = buf_k[slot]
    vc = buf_v[slot]
    s = jnp.where(head0, _dot_nt(qb, kc[:, lo]), _dot_nt(qb, kc[:, hi])) * B_SCALE + bias
    s = jnp.where(sel, s, NEG)
    m = m_scr[...]
    m_new = jnp.maximum(m, jnp.max(s, axis=-1, keepdims=True))
    p = jnp.where(sel, jnp.exp(s - m_new), 0.0)
    alpha = jnp.exp(m - m_new)
    m_scr[...] = m_new
    l_scr[...] = alpha * l_scr[...] + jnp.sum(p, axis=-1, keepdims=True)
    acc_scr[...] = alpha * acc_scr[...] + jnp.where(head0, _dot(p, vc[:, lo]), _dot(p, vc[:, hi]))

    @pl.when(c == n_chunks - 1)
    def _():
        o = acc_scr[...] / l_scr[...]
        for hh in range(B_HEADS):
            cols = slice(hh * B_HD, (hh + 1) * B_HD)
            out_ref[0, :, cols] = o[hh:hh + 1] * _silu(gb_ref[0, :, cols])


def _sample_pass2(page_table, cfar, qb, kbn, vbn, gate_b, score, thr, tail, pool_k, pool_v,
                  pages_per_chunk):
    nq, n_pages = page_table.shape
    page = pool_k.shape[2]
    past = n_pages * page
    n_chunks = n_pages // pages_per_chunk
    ck = pages_per_chunk * page
    per_q = lambda a: pl.BlockSpec((1,) + a.shape[1:], lambda g, pt: (g // n_chunks, 0, 0))
    any_spec = pl.BlockSpec(memory_space=pl.ANY)
    grid_spec = pltpu.PrefetchScalarGridSpec(
        num_scalar_prefetch=1,
        grid=(nq * n_chunks,),
        in_specs=[pl.BlockSpec(cfar.shape, lambda g, pt: (0, 0)),
                  per_q(qb), per_q(kbn), per_q(vbn), per_q(gate_b), per_q(score), per_q(thr),
                  pl.BlockSpec(tail.shape, lambda g, pt: (0, 0)),
                  any_spec, any_spec],
        out_specs=pl.BlockSpec((1, 1, B_WIDTH), lambda g, pt: (g // n_chunks, 0, 0)),
        scratch_shapes=[pltpu.VMEM((2, ck, B_KVW), F32), pltpu.VMEM((2, ck, B_KVW), F32),
                        pltpu.SemaphoreType.DMA((2, 2)),
                        pltpu.VMEM((B_HEADS, 1), F32), pltpu.VMEM((B_HEADS, 1), F32),
                        pltpu.VMEM((B_HEADS, B_HD), F32)],
    )
    return pl.pallas_call(
        functools.partial(_sample_pass2_kernel, n_chunks=n_chunks, pages_per_chunk=pages_per_chunk,
                          page=page, past=past),
        grid_spec=grid_spec,
        out_shape=jax.ShapeDtypeStruct((nq, 1, B_WIDTH), F32),
        compiler_params=_params(("arbitrary",)),
    )(page_table, cfar, qb, kbn, vbn, gate_b, score, thr, tail, pool_k, pool_v)


def _prep_w_in(w_in):
    offs = np.concatenate([[0], np.cumsum(IN_SIZES)])
    part = lambda k: w_in[:, offs[k]:offs[k + 1]]
    c_q, c_kv, k_pe, gate_a, q_b, k_b, v_b, gate_b, q_idx, k_idx, w_idx = [part(k) for k in range(11)]
    pad = jnp.zeros((w_in.shape[0], LANES - IDX_HEADS), w_in.dtype)
    cols = [c_q, c_kv, gate_a, q_b, k_b, v_b, gate_b, q_idx, k_pe, k_idx, w_idx, pad]
    return jnp.concatenate(cols, axis=1).astype(BF16)


def _prep_w_uq(w_uq):
    w = w_uq.reshape(w_uq.shape[0], A_HEADS, A_NOPE + A_ROPE)
    nope = w[:, :, :A_NOPE].reshape(w_uq.shape[0], A_HEADS * A_NOPE)
    rope = w[:, :, A_NOPE:].reshape(w_uq.shape[0], A_HEADS * A_ROPE)
    return jnp.concatenate([nope, rope], axis=1).astype(BF16)


def kernel(x_prompt, x_sample, p_prompt, p_sample, cache_ckv, cache_kpe, cache_k, cache_v, cache_kidx,
           page_table, rel_table, g_attn, w_in, g_q, w_uq, g_kv, w_uk, w_uv, w_out, g_ple, w_ple_gate,
           w_ple_proj, g_final):
    nb, seq, d = x_prompt.shape
    nq, nt, _ = x_sample.shape
    depth = w_in.shape[0]
    assert depth == 1 and nt == 1
    n_pages = page_table.shape[1]
    page = cache_ckv.shape[2]
    past = n_pages * page
    tq = min(256, seq)
    tm_s = min(128, nq)
    pages_per_chunk = min(32, n_pages)
    assert seq % tq == 0 and nq % tm_s == 0 and n_pages % pages_per_chunk == 0
    assert tq + 1 > REL_MAX_DIST

    row2 = lambda v: v.reshape(1, -1)
    w_in_p = _prep_w_in(w_in[0])
    w_uq_p = _prep_w_uq(w_uq[0])
    w_uk_b = w_uk[0].astype(BF16)
    w_uv_b = w_uv[0].astype(BF16)
    w_out_b = w_out[0].astype(BF16)
    w_gate_b = w_ple_gate[0].astype(BF16)
    w_proj_b = w_ple_proj[0].astype(BF16)
    cfar = rel_table[REL_BUCKETS - 1]

    tabs_p = _rope_tables(jnp.arange(seq, dtype=I32))
    xp = x_prompt.reshape(nb * seq, d)
    (qcat, qi2, _, _, wi, qb, ga, gb, ckv, kpe, kb, vb, kidx, kcat, ii, kbb, vbb) = _proj_in(
        xp, tabs_p, seq // tq, row2(g_attn[0]), w_in_p, row2(g_q[0]), w_uq_p, row2(g_kv[0]), w_uk_b, tq)
    b3 = lambda a: a.reshape((nb, seq) + a.shape[1:])
    b4 = lambda a: a.reshape((a.shape[0], nb, seq) + a.shape[2:])
    ma = _mla_prompt(b4(qcat), b3(kcat), b3(ga), w_uv_b, tq)
    rel = np.arange(tq)[:, None] + tq - np.arange(2 * tq)[None, :]
    bias = _bias_tiles(rel_table, rel)
    k_top = min(IDX_TOPK, seq // 4)
    mb = _dsa_prompt(cfar, b4(qi2), b3(wi), b3(ii), b3(qb), b3(kbb), b3(vbb), b3(gb), bias, tq, k_top)
    y_prompt = _out_proj(xp, ma.reshape(nb * seq, -1), mb.reshape(nb * seq, -1),
                         p_prompt[0].reshape(nb * seq, -1), w_out_b, row2(g_ple[0]), w_gate_b, w_proj_b,
                         row2(g_final), tq).reshape(nb, seq, d)

    tabs_s = _rope_tables(jnp.full((tm_s,), past, I32))
    xs = x_sample.reshape(nq, d)
    (qcat_s, _, qpe_s, qidx_s, wi_s, qb_s, ga_s, gb_s, ckv_s, kpe_s, kb_s, vb_s, kidx_s, _, _, _, _) = _proj_in(
        xs, tabs_s, 1, row2(g_attn[0]), w_in_p, row2(g_q[0]), w_uq_p, row2(g_kv[0]), w_uk_b, tm_s)
    qlat_s = jnp.swapaxes(qcat_s[:, :, :KV_LORA], 0, 1)
    ma_s, score = _sample_pass1(
        page_table, qlat_s, qpe_s.reshape(nq, A_HEADS, A_ROPE), qidx_s.reshape(nq, IDX_HEADS, IDX_DIM),
        wi_s[:, :IDX_HEADS].reshape(nq, IDX_HEADS, 1), ckv_s.reshape(nq, 1, -1), kpe_s.reshape(nq, 1, -1),
        kidx_s.reshape(nq, 1, -1), ga_s.reshape(nq, 1, -1), w_uv_b, cache_ckv, cache_kpe, cache_kidx,
        pages_per_chunk)
    k_top_s = min(IDX_TOPK, (past + nt) // 4)
    thr = _topk_thresh(score.reshape(nq, n_pages + SCORE_PAD // LANES, LANES), k_top_s)
    rel_tail = np.broadcast_to(np.concatenate(
        [page - np.arange(page), np.zeros(LANES, np.int64)])[None, :], (8, page + LANES))
    tail = _bias_tiles(rel_table, rel_tail)[:, 0, :]
    mb_s = _sample_pass2(page_table, cfar.reshape(B_HEADS, 1), qb_s.reshape(nq, B_HEADS, B_HD),
                         kb_s.reshape(nq, 1, -1),
                         vb_s.reshape(nq, 1, -1), gb_s.reshape(nq, 1, -1), score, thr, tail,
                         cache_k.reshape(cache_k.shape[:3] + (-1,)), cache_v.reshape(cache_v.shape[:3] + (-1,)),
                         pages_per_chunk)
    y_sample = _out_proj(xs, ma_s.reshape(nq, -1).astype(BF16), mb_s.reshape(nq, -1).astype(BF16),
                         p_sample[0].reshape(nq, -1), w_out_b, row2(g_ple[0]), w_gate_b, w_proj_b,
                         row2(g_final), tm_s).reshape(nq, nt, d)

    return (y_prompt, y_sample,
            ckv.reshape(1, nb, seq, -1), kpe.reshape(1, nb, seq, -1),
            kb.reshape(1, nb, seq, B_KV_HEADS, B_HD), vb.reshape(1, nb, seq, B_KV_HEADS, B_HD),
            kidx.reshape(1, nb, seq, -1),
            ckv_s.reshape(1, nq, nt, -1), kpe_s.reshape(1, nq, nt, -1),
            kb_s.reshape(1, nq, nt, B_KV_HEADS, B_HD), vb_s.reshape(1, nq, nt, B_KV_HEADS, B_HD),
            kidx_s.reshape(1, nq, nt, -1))
```

```python
import functools
import math

import numpy as np
import jax
import jax.numpy as jnp
from jax import lax
from jax.experimental import pallas as pl
from jax.experimental.pallas import tpu as pltpu

F32 = jnp.float32
BF16 = jnp.bfloat16
I32 = jnp.int32

A_HEADS = 8
A_NOPE = 128
A_ROPE = 64
A_V = 128
A_WIDTH = A_HEADS * A_V
Q_LORA = 512
KV_LORA = 256
B_HEADS = 8
B_KV_HEADS = 2
B_HD = 128
B_GROUP = B_HEADS // B_KV_HEADS
B_WIDTH = B_HEADS * B_HD
B_KVW = B_KV_HEADS * B_HD
IDX_HEADS = 16
IDX_DIM = 64
IDX_ROPE = 32
IDX_TOPK = 256
REL_BUCKETS = 32
REL_MAX_DIST = 128
ROPE_THETA = 10000.0
EPS = 1e-6
MLA_SCALE = (A_NOPE + A_ROPE) ** -0.5
B_SCALE = B_HD ** -0.5
IDX_W_SCALE = (IDX_HEADS ** -0.5) * (IDX_DIM ** -0.5)
IN_SIZES = (Q_LORA, KV_LORA, A_ROPE, A_WIDTH, B_WIDTH, B_KVW, B_KVW, B_WIDTH,
            IDX_HEADS * IDX_DIM, IDX_DIM, IDX_HEADS)

LANES = 128
NEG = -1e30
VMEM_LIMIT = 56 * 1024 * 1024
INT_MIN = -2 ** 31

C_Q = 0
C_KV = C_Q + Q_LORA
C_GA = C_KV + KV_LORA
C_QB = C_GA + A_WIDTH
C_KB = C_QB + B_WIDTH
C_VB = C_KB + B_KVW
C_GB = C_VB + B_KVW
C_QI = C_GB + B_WIDTH
C_SLAB = C_QI + IDX_HEADS * IDX_DIM
C_WI = C_SLAB + LANES
NP_IN = C_WI + LANES
QCAT = KV_LORA + LANES
N_TAB = 11
SCORE_PAD = 8 * LANES


def _dot(a, b):
    return jnp.dot(a, b, preferred_element_type=F32)


def _dot_nt(a, b):
    return lax.dot_general(a, b, (((1,), (1,)), ((), ())), preferred_element_type=F32)


def _silu(x):
    return x / (1.0 + jnp.exp(-x))


def _params(sem=None):
    return pltpu.CompilerParams(dimension_semantics=sem, vmem_limit_bytes=VMEM_LIMIT)


def _resident(shape):
    nd = len(shape)
    return pl.BlockSpec(shape, lambda *_: (0,) * nd, pipeline_mode=pl.Buffered(1))


def _rope_tables(pos):
    def cs(half):
        inv = ROPE_THETA ** (-jnp.arange(half, dtype=F32) / half)
        ang = pos.astype(F32)[:, None] * inv[None, :]
        return jnp.cos(ang), jnp.sin(ang)
    n = pos.shape[0]
    c32, s32 = cs(A_ROPE // 2)
    c16, s16 = cs(IDX_ROPE // 2)
    z16 = jnp.zeros((n, 16), F32)
    z32 = jnp.zeros((n, 32), F32)
    z64 = jnp.zeros((n, 64), F32)
    o32 = jnp.ones((n, 32), F32)
    cat = lambda *a: jnp.concatenate(a, axis=1)
    a64, bm64, bp64 = cat(c32, c32), cat(-s32, z32), cat(z32, s32)
    a16, cm16, cp16 = cat(c16, c16, o32), cat(-s16, z16, z32), cat(z16, s16, z32)
    tabs = [cat(a64, a64), cat(bm64, bm64), cat(bp64, bp64),
            cat(a16, a16), cat(cm16, cm16), cat(cp16, cp16),
            cat(a64, a16), cat(bm64, z64), cat(bp64, z64), cat(z64, cm16), cat(z64, cp16)]
    return jnp.concatenate(tabs, axis=1)


def _proj_in_kernel(x_ref, gattn_ref, win_ref, gq_ref, wuq_ref, gkv_ref, wuk_ref, tab_ref,
                    qcat_ref, qi2_ref, qpe_ref, qidx_ref, wi_ref, qb_ref, ga_ref, gb_ref,
                    ckv_ref, kpe_ref, kb_ref, vb_ref, kidx_ref, kcat_ref, ii_ref, kbb_ref, vbb_ref,
                    h_scr):
    x = x_ref[...]
    ms = jnp.mean(x * x, axis=-1, keepdims=True)
    h_scr[...] = (x * lax.rsqrt(ms + EPS) * gattn_ref[...]).astype(BF16)

    def zcols(c0, width):
        return _dot(h_scr[...], win_ref[:, c0:c0 + width])

    def tab(k):
        return tab_ref[:, k * LANES:(k + 1) * LANES]

    tm = x.shape[0]
    lane = lax.broadcasted_iota(I32, (tm, LANES), 1)
    lo = lane < 64

    def rope(xg, a, bm, bp, shift):
        return (xg * tab(a) + pltpu.roll(xg, LANES - shift, 1) * tab(bm)
                + pltpu.roll(xg, shift, 1) * tab(bp))

    cq = zcols(C_Q, Q_LORA)
    cq = cq * lax.rsqrt(jnp.mean(cq * cq, axis=-1, keepdims=True) + EPS) * gq_ref[...]
    q = _dot(cq.astype(BF16), wuq_ref[...])
    n_nope = A_HEADS * A_NOPE
    for g in range(A_HEADS // 2):
        xg = q[:, n_nope + g * LANES:n_nope + (g + 1) * LANES]
        rg = rope(xg, 0, 1, 2, A_ROPE // 2)
        qpe_ref[:, g * LANES:(g + 1) * LANES] = rg.astype(BF16)
        qcat_ref[2 * g, :, KV_LORA:] = jnp.where(lo, rg, 0.0).astype(BF16)
        qcat_ref[2 * g + 1, :, KV_LORA:] = jnp.where(lo, 0.0, rg).astype(BF16)
    for hh in range(A_HEADS):
        qn = q[:, hh * A_NOPE:(hh + 1) * A_NOPE].astype(BF16)
        qcat_ref[hh, :, :KV_LORA] = _dot(qn, wuk_ref[hh]).astype(BF16)

    ckv = zcols(C_KV, KV_LORA)
    ckv = ckv * lax.rsqrt(jnp.mean(ckv * ckv, axis=-1, keepdims=True) + EPS) * gkv_ref[...]
    ckv_ref[...] = ckv
    kcat_ref[:, :KV_LORA] = ckv.astype(BF16)

    slab = zcols(C_SLAB, LANES)
    slab = (slab * tab(6) + pltpu.roll(slab, LANES - 32, 1) * tab(7) + pltpu.roll(slab, 32, 1) * tab(8)
            + pltpu.roll(slab, LANES - 16, 1) * tab(9) + pltpu.roll(slab, 16, 1) * tab(10))
    swapped = pltpu.roll(slab, 64, 1)
    kpe_ref[...] = slab[:, :A_ROPE]
    kidx_ref[...] = swapped[:, :IDX_DIM]
    kcat_ref[:, KV_LORA:] = jnp.where(lo, slab, swapped).astype(BF16)
    ii_ref[...] = jnp.where(lo, swapped, slab).astype(BF16)

    kb = zcols(C_KB, B_KVW)
    kb_ref[...] = kb
    kbb_ref[...] = kb.astype(BF16)
    vb = zcols(C_VB, B_KVW)
    vb_ref[...] = vb
    vbb_ref[...] = vb.astype(BF16)
    for c in range(B_WIDTH // 512):
        qb_ref[:, c * 512:(c + 1) * 512] = zcols(C_QB + c * 512, 512).astype(BF16)
        ga_ref[:, c * 512:(c + 1) * 512] = zcols(C_GA + c * 512, 512)
        gb_ref[:, c * 512:(c + 1) * 512] = zcols(C_GB + c * 512, 512)

    for c in range(IDX_HEADS * IDX_DIM // 512):
        qi = zcols(C_QI + c * 512, 512)
        for g in range(4):
            rg = rope(qi[:, g * LANES:(g + 1) * LANES], 3, 4, 5, IDX_ROPE // 2)
            col = c * 512 + g * LANES
            qidx_ref[:, col:col + LANES] = rg.astype(BF16)
            hh = col // IDX_DIM
            qi2_ref[hh] = jnp.where(lo, rg, 0.0).astype(BF16)
            qi2_ref[hh + 1] = jnp.where(lo, 0.0, rg).astype(BF16)
    wi_ref[...] = zcols(C_WI, LANES) * IDX_W_SCALE


def _proj_in(x, tabs, tab_blocks, g_attn, w_in_p, g_q, w_uq_p, g_kv, w_uk, tm):
    n, d = x.shape
    grid = (n // tm,)
    row = lambda w: pl.BlockSpec((tm, w), lambda i: (i, 0))
    head = lambda nh, w: pl.BlockSpec((nh, tm, w), lambda i: (0, i, 0))
    out_shape = [
        jax.ShapeDtypeStruct((A_HEADS, n, QCAT), BF16),
        jax.ShapeDtypeStruct((IDX_HEADS, n, LANES), BF16),
        jax.ShapeDtypeStruct((n, A_HEADS * A_ROPE), BF16),
        jax.ShapeDtypeStruct((n, IDX_HEADS * IDX_DIM), BF16),
        jax.ShapeDtypeStruct((n, LANES), F32),
        jax.ShapeDtypeStruct((n, B_WIDTH), BF16),
        jax.ShapeDtypeStruct((n, A_WIDTH), F32),
        jax.ShapeDtypeStruct((n, B_WIDTH), F32),
        jax.ShapeDtypeStruct((n, KV_LORA), F32),
        jax.ShapeDtypeStruct((n, A_ROPE), F32),
        jax.ShapeDtypeStruct((n, B_KVW), F32),
        jax.ShapeDtypeStruct((n, B_KVW), F32),
        jax.ShapeDtypeStruct((n, IDX_DIM), F32),
        jax.ShapeDtypeStruct((n, QCAT), BF16),
        jax.ShapeDtypeStruct((n, LANES), BF16),
        jax.ShapeDtypeStruct((n, B_KVW), BF16),
        jax.ShapeDtypeStruct((n, B_KVW), BF16),
    ]
    out_specs = [head(A_HEADS, QCAT), head(IDX_HEADS, LANES), row(A_HEADS * A_ROPE),
                 row(IDX_HEADS * IDX_DIM), row(LANES), row(B_WIDTH), row(A_WIDTH), row(B_WIDTH),
                 row(KV_LORA), row(A_ROPE), row(B_KVW), row(B_KVW), row(IDX_DIM),
                 row(QCAT), row(LANES), row(B_KVW), row(B_KVW)]
    in_specs = [row(d), _resident(g_attn.shape), _resident(w_in_p.shape), _resident(g_q.shape),
                _resident(w_uq_p.shape), _resident(g_kv.shape), _resident(w_uk.shape),
                pl.BlockSpec((tm, N_TAB * LANES), lambda i: (i % tab_blocks, 0))]
    return pl.pallas_call(
        _proj_in_kernel, grid=grid, in_specs=in_specs, out_specs=out_specs, out_shape=out_shape,
        scratch_shapes=[pltpu.VMEM((tm, d), BF16)],
        compiler_params=_params(("parallel",)),
    )(x, g_attn, w_in_p, g_q, w_uq_p, g_kv, w_uk, tabs)


def _flash_update(carry, s, valid, v):
    m, l, acc = carry
    if valid is not None:
        s = jnp.where(valid, s, NEG)
    m_new = jnp.maximum(m, jnp.max(s, axis=-1, keepdims=True))
    p = jnp.exp(s - m_new)
    if valid is not None:
        p = jnp.where(valid, p, 0.0)
    alpha = jnp.exp(m - m_new)
    l = alpha * l + jnp.sum(p, axis=-1, keepdims=True)
    acc = alpha * acc + _dot(p.astype(v.dtype), v)
    return m_new, l, acc


def _flash_init(m, d):
    return (jnp.full((m, 1), NEG, F32), jnp.zeros((m, 1), F32), jnp.zeros((m, d), F32))


def _flash_reset(m_scr, l_scr, acc_scr):
    m_scr[...] = jnp.full(m_scr.shape, NEG, F32)
    l_scr[...] = jnp.zeros(l_scr.shape, F32)
    acc_scr[...] = jnp.zeros(acc_scr.shape, F32)


def _flash_step(m_scr, l_scr, acc_scr, hh, s, valid, v):
    m, l, acc = _flash_update((m_scr[hh], l_scr[hh], acc_scr[hh]), s, valid, v)
    m_scr[hh] = m
    l_scr[hh] = l
    acc_scr[hh] = acc


def _mla_prompt_kernel(qcat_ref, kcat_ref, ga_ref, wuv_ref, out_ref, m_scr, l_scr, acc_scr, *, tq):
    i = pl.program_id(1)
    row = lax.broadcasted_iota(I32, (tq, tq), 0)
    col = lax.broadcasted_iota(I32, (tq, tq), 1)
    causal = col <= row
    _flash_reset(m_scr, l_scr, acc_scr)

    def chunk(j, valid):
        kc = kcat_ref[0, pl.ds(pl.multiple_of(j * tq, tq), tq), :]
        for hh in range(A_HEADS):
            s = _dot_nt(qcat_ref[hh, 0], kc) * MLA_SCALE
            _flash_step(m_scr, l_scr, acc_scr, hh, s, valid, kc[:, :KV_LORA])

    def far(j, carry):
        chunk(j, None)
        return carry

    lax.fori_loop(0, i, far, 0)
    chunk(i, causal)
    for hh in range(A_HEADS):
        lat = (acc_scr[hh] / l_scr[hh]).astype(BF16)
        o = _dot(lat, wuv_ref[hh])
        cols = slice(hh * A_V, (hh + 1) * A_V)
        out_ref[0, :, cols] = (o * _silu(ga_ref[0, :, cols])).astype(BF16)


def _mla_prompt(qcat, kcat, gate_a, w_uv, tq):
    nb, seq = kcat.shape[0], kcat.shape[1]
    return pl.pallas_call(
        functools.partial(_mla_prompt_kernel, tq=tq),
        grid=(nb, seq // tq),
        in_specs=[pl.BlockSpec((A_HEADS, 1, tq, QCAT), lambda b, i: (0, b, i, 0)),
                  pl.BlockSpec((1, seq, QCAT), lambda b, i: (b, 0, 0)),
                  pl.BlockSpec((1, tq, A_WIDTH), lambda b, i: (b, i, 0)),
                  _resident(w_uv.shape)],
        out_specs=pl.BlockSpec((1, tq, A_WIDTH), lambda b, i: (b, i, 0)),
        out_shape=jax.ShapeDtypeStruct((nb, seq, A_WIDTH), BF16),
        scratch_shapes=[pltpu.VMEM((A_HEADS, tq, 1), F32), pltpu.VMEM((A_HEADS, tq, 1), F32),
                        pltpu.VMEM((A_HEADS, tq, KV_LORA), F32)],
        compiler_params=_params(("parallel", "parallel")),
    )(qcat, kcat, gate_a, w_uv)


def _sort_key(score):
    b = pltpu.bitcast(score + 0.0, I32)
    return jnp.where(b < 0, b ^ 0x7FFFFFFF, b)


def _kth_largest_key(count_ge, shape, k):
    def body(it, lo):
        bit = 31 - it
        cand = lo + lax.shift_left(jnp.int32(1), bit)
        return jnp.where(count_ge(cand) >= k, cand, lo)
    return lax.fori_loop(0, 32, body, jnp.full(shape, INT_MIN, I32))


def _dsa_prompt_kernel(cfar_ref, qi2_ref, wi_ref, ii_ref, qb_ref, kbb_ref, vbb_ref, gb_ref, bias_ref,
                       out_ref, key_scr, thr_scr, m_scr, l_scr, acc_scr, *, tq, seq, k_top, cw):
    i = pl.program_id(1)
    t0 = i * tq
    w = wi_ref[0]
    qrow = t0 + lax.broadcasted_iota(I32, (tq, cw), 0)
    kcol = lax.broadcasted_iota(I32, (tq, cw), 1)

    for c in range(seq // cw):
        @pl.when(c * cw < t0 + tq)
        def _():
            kc = ii_ref[0, c * cw:(c + 1) * cw, :]
            acc = jnp.zeros((tq, cw), F32)
            for hh in range(IDX_HEADS):
                s = _dot_nt(qi2_ref[hh, 0], kc)
                acc = acc + w[:, hh:hh + 1] * jnp.maximum(s, 0.0)
            acc = jnp.where(kcol + c * cw <= qrow, acc, -jnp.inf)
            key = _sort_key(acc)
            for u in range(cw // tq):
                key_scr[c * (cw // tq) + u] = key[:, u * tq:(u + 1) * tq]

    for n in range(1, seq // tq + 1):
        @pl.when(i == n - 1)
        def _():
            def count_ge(cand):
                part = jnp.zeros((tq, LANES), F32)
                for j in range(n):
                    ge = (key_scr[j] >= cand).astype(F32)
                    for u in range(tq // LANES):
                        part = part + ge[:, u * LANES:(u + 1) * LANES]
                return jnp.sum(part, axis=1, keepdims=True)

            thr_scr[...] = _kth_largest_key(count_ge, (tq, 1), k_top)

    thr = thr_scr[...]

    row = lax.broadcasted_iota(I32, (tq, tq), 0)
    col = lax.broadcasted_iota(I32, (tq, tq), 1)
    causal = col <= row
    _flash_reset(m_scr, l_scr, acc_scr)

    def chunk(j, bias_of, extra):
        off = pl.multiple_of(j * tq, tq)
        sel = key_scr[j] >= thr
        if extra is not None:
            sel = sel & extra
        for hh in range(B_HEADS):
            kv_cols = slice((hh // B_GROUP) * B_HD, (hh // B_GROUP + 1) * B_HD)
            qh = qb_ref[0, :, hh * B_HD:(hh + 1) * B_HD]
            s = _dot_nt(qh, kbb_ref[0, pl.ds(off, tq), kv_cols]) * B_SCALE + bias_of(hh)
            _flash_step(m_scr, l_scr, acc_scr, hh, s, sel, vbb_ref[0, pl.ds(off, tq), kv_cols])

    def far(j, carry):
        chunk(j, lambda hh: cfar_ref[hh], None)
        return carry

    lax.fori_loop(0, jnp.maximum(i - 1, 0), far, 0)

    @pl.when(i > 0)
    def _():
        chunk(i - 1, lambda hh: bias_ref[hh, :, :tq], None)

    chunk(i, lambda hh: bias_ref[hh, :, tq:], causal)
    for hh in range(B_HEADS):
        cols = slice(hh * B_HD, (hh + 1) * B_HD)
        out_ref[0, :, cols] = ((acc_scr[hh] / l_scr[hh]) * _silu(gb_ref[0, :, cols])).astype(BF16)


def _dsa_prompt(cfar, qi2, wi, ii, qb, kbb, vbb, gate_b, bias, tq, k_top):
    nb, seq = ii.shape[0], ii.shape[1]
    cw = min(512, seq)
    assert cw % tq == 0 and seq % cw == 0 and k_top <= tq
    full = lambda w: pl.BlockSpec((1, seq, w), lambda b, i: (b, 0, 0))
    tile = lambda w: pl.BlockSpec((1, tq, w), lambda b, i: (b, i, 0))
    return pl.pallas_call(
        functools.partial(_dsa_prompt_kernel, tq=tq, seq=seq, k_top=k_top, cw=cw),
        grid=(nb, seq // tq),
        in_specs=[pl.BlockSpec(memory_space=pltpu.SMEM),
                  pl.BlockSpec((IDX_HEADS, 1, tq, LANES), lambda b, i: (0, b, i, 0)),
                  tile(LANES), full(LANES), tile(B_WIDTH), full(B_KVW), full(B_KVW), tile(B_WIDTH),
                  _resident(bias.shape)],
        out_specs=tile(B_WIDTH),
        out_shape=jax.ShapeDtypeStruct((nb, seq, B_WIDTH), BF16),
        scratch_shapes=[pltpu.VMEM((seq // tq, tq, tq), I32), pltpu.VMEM((tq, 1), I32),
                        pltpu.VMEM((B_HEADS, tq, 1), F32), pltpu.VMEM((B_HEADS, tq, 1), F32),
                        pltpu.VMEM((B_HEADS, tq, B_HD), F32)],
        compiler_params=_params(("parallel", "parallel")),
    )(cfar, qi2, wi, ii, qb, kbb, vbb, gate_b, bias)


def _bias_kernel(table_ref, bucket_ref, out_ref):
    bucket = bucket_ref[...]
    for hh in range(B_HEADS):
        acc = jnp.zeros(bucket.shape, F32)
        for b in range(REL_BUCKETS):
            acc = jnp.where(bucket == b, table_ref[b, hh], acc)
        out_ref[hh] = acc


def _t5_bucket_np(rel):
    n = np.maximum(rel, 0)
    exact = REL_BUCKETS // 2
    nf = np.maximum(n, 1).astype(np.float32)
    large = exact + (np.log(nf / exact) / math.log(REL_MAX_DIST / exact)
                     * (REL_BUCKETS - exact)).astype(np.int32)
    return np.where(n < exact, n, np.minimum(large, REL_BUCKETS - 1)).astype(np.int32)


def _bias_tiles(rel_table, rel):
    bucket = jnp.asarray(_t5_bucket_np(rel))
    return pl.pallas_call(
        _bias_kernel,
        in_specs=[pl.BlockSpec(memory_space=pltpu.SMEM), pl.BlockSpec(memory_space=pltpu.VMEM)],
        out_specs=pl.BlockSpec(memory_space=pltpu.VMEM),
        out_shape=jax.ShapeDtypeStruct((B_HEADS,) + rel.shape, F32),
    )(rel_table, bucket)


def _out_proj_kernel(x_ref, ma_ref, mb_ref, p_ref, wout_ref, gple_ref, wgate_ref, wproj_ref, gfin_ref,
                     y_ref, x1_scr, hn_scr, *, cw):
    d = x_ref.shape[1]
    nc = d // cw
    half = ma_ref.shape[1]
    ss = jnp.zeros((x_ref.shape[0], 1), F32)
    for c in range(nc):
        cols = slice(c * cw, (c + 1) * cw)
        x1 = (x_ref[:, cols] + _dot(ma_ref[...], wout_ref[:half, cols])
              + _dot(mb_ref[...], wout_ref[half:, cols]))
        x1_scr[:, cols] = x1
        ss = ss + jnp.sum(x1 * x1, axis=-1, keepdims=True)
    rs = lax.rsqrt(ss / d + EPS)
    for c in range(nc):
        cols = slice(c * cw, (c + 1) * cw)
        hn_scr[:, cols] = (x1_scr[:, cols] * rs * gple_ref[:, cols]).astype(BF16)
    pb = p_ref[...].astype(BF16)
    ss = jnp.zeros((x_ref.shape[0], 1), F32)
    for c in range(nc):
        cols = slice(c * cw, (c + 1) * cw)
        gt = 1.0 / (1.0 + jnp.exp(-_dot(hn_scr[...], wgate_ref[:, cols])))
        x2 = x1_scr[:, cols] + gt * _dot(pb, wproj_ref[:, cols])
        x1_scr[:, cols] = x2
        ss = ss + jnp.sum(x2 * x2, axis=-1, keepdims=True)
    rs = lax.rsqrt(ss / d + EPS)
    for c in range(nc):
        cols = slice(c * cw, (c + 1) * cw)
        y_ref[:, cols] = x1_scr[:, cols] * rs * gfin_ref[:, cols]


def _out_proj(x, ma, mb, p, w_out, g_ple, w_gate, w_proj, g_final, tm):
    n, d = x.shape
    row = lambda w: pl.BlockSpec((tm, w), lambda i: (i, 0))
    return pl.pallas_call(
        functools.partial(_out_proj_kernel, cw=min(512, d)),
        grid=(n // tm,),
        in_specs=[row(d), row(ma.shape[1]), row(mb.shape[1]), row(p.shape[1]),
                  _resident(w_out.shape), _resident(g_ple.shape), _resident(w_gate.shape),
                  _resident(w_proj.shape), _resident(g_final.shape)],
        out_specs=row(d),
        out_shape=jax.ShapeDtypeStruct((n, d), F32),
        scratch_shapes=[pltpu.VMEM((tm, d), F32), pltpu.VMEM((tm, d), BF16)],
        compiler_params=_params(("parallel",)),
    )(x, ma, mb, p, w_out, g_ple, w_gate, w_proj, g_final)


def _page_copies(pt_ref, pools, bufs, lane_major, sems, q, chunk, slot, pages_per_chunk):
    copies = []
    for k, (pool, buf, lm) in enumerate(zip(pools, bufs, lane_major)):
        rows, width = pool.shape[2], pool.shape[3]
        for p in range(pages_per_chunk):
            pid = pt_ref[q, chunk * pages_per_chunk + p]
            if lm:
                dst = buf.at[slot, :, pl.ds(p * width, width)]
            else:
                dst = buf.at[slot, pl.ds(p * rows, rows)]
            copies.append(pltpu.make_async_copy(pool.at[0, pid], dst, sems.at[slot, k]))
    return copies


def _sample_pass1_kernel(pt_ref, qlat_ref, qpe_ref, qidx_ref, wi_ref, ckvn_ref, kpen_ref, kidxn_ref,
                         ga_ref, wuv_ref, pool_ckv, pool_kpe, pool_kidx,
                         out_ref, score_ref,
                         buf_ckv, buf_kpe, buf_kidx, sems, m_scr, l_scr, acc_scr,
                         *, n_chunks, pages_per_chunk, page, past):
    g = pl.program_id(0)
    n_steps = pl.num_programs(0)
    q = g // n_chunks
    c = g % n_chunks
    slot = g % 2
    pools = (pool_ckv, pool_kpe, pool_kidx)
    bufs = (buf_ckv, buf_kpe, buf_kidx)
    ck = pages_per_chunk * page

    def copies(step, slot_):
        return _page_copies(pt_ref, pools, bufs, (False, True, True), sems, step // n_chunks,
                            step % n_chunks, slot_, pages_per_chunk)

    @pl.when(g == 0)
    def _():
        for cp in copies(g, slot):
            cp.start()

    @pl.when(g + 1 < n_steps)
    def _():
        for cp in copies(g + 1, 1 - slot):
            cp.start()

    qlat = qlat_ref[0].astype(F32)
    qpe = qpe_ref[0].astype(F32)
    qidx = qidx_ref[0].astype(F32)
    wcol = wi_ref[0]

    @pl.when(c == 0)
    def _():
        ckvn = ckvn_ref[0]
        s_new = (jnp.sum(qlat * ckvn, axis=-1, keepdims=True)
                 + jnp.sum(qpe * kpen_ref[0], axis=-1, keepdims=True)) * MLA_SCALE
        m_scr[...] = s_new
        l_scr[...] = jnp.ones_like(s_new)
        acc_scr[...] = jnp.broadcast_to(ckvn, acc_scr.shape)
        si = jnp.maximum(jnp.sum(qidx * kidxn_ref[0], axis=-1, keepdims=True), 0.0) * wcol
        si = jnp.sum(si, axis=0, keepdims=True)
        lane = lax.broadcasted_iota(I32, (1, SCORE_PAD), 1)
        score_ref[0, :, past:] = jnp.where(lane == 0, si, -jnp.inf)

    for cp in copies(g, slot):
        cp.wait()

    kc = buf_ckv[slot]
    s = (_dot_nt(qlat, kc) + _dot(qpe, buf_kpe[slot])) * MLA_SCALE
    m, l, acc = _flash_update((m_scr[...], l_scr[...], acc_scr[...]), s, None, kc)
    m_scr[...] = m
    l_scr[...] = l
    acc_scr[...] = acc

    si = jnp.maximum(_dot(qidx, buf_kidx[slot]), 0.0) * wcol
    score_ref[0, :, pl.ds(pl.multiple_of(c * ck, ck), ck)] = jnp.sum(si, axis=0, keepdims=True)

    @pl.when(c == n_chunks - 1)
    def _():
        lat = (acc / l).astype(BF16)
        for hh in range(A_HEADS):
            o = _dot(lat, wuv_ref[hh])[hh:hh + 1]
            cols = slice(hh * A_V, (hh + 1) * A_V)
            out_ref[0, :, cols] = o * _silu(ga_ref[0, :, cols])


def _sample_pass1(page_table, qlat, qpe, qidx, wi, ckvn, kpen, kidxn, gate_a, w_uv,
                  pool_ckv, pool_kpe, pool_kidx, pages_per_chunk):
    nq, n_pages = page_table.shape
    page = pool_ckv.shape[2]
    past = n_pages * page
    n_chunks = n_pages // pages_per_chunk
    ck = pages_per_chunk * page
    per_q = lambda a: pl.BlockSpec((1,) + a.shape[1:], lambda g, pt: (g // n_chunks, 0, 0))
    any_spec = pl.BlockSpec(memory_space=pl.ANY)
    grid_spec = pltpu.PrefetchScalarGridSpec(
        num_scalar_prefetch=1,
        grid=(nq * n_chunks,),
        in_specs=[per_q(qlat), per_q(qpe), per_q(qidx), per_q(wi), per_q(ckvn), per_q(kpen),
                  per_q(kidxn), per_q(gate_a),
                  pl.BlockSpec(w_uv.shape, lambda g, pt: (0, 0, 0), pipeline_mode=pl.Buffered(1)),
                  any_spec, any_spec, any_spec],
        out_specs=[pl.BlockSpec((1, 1, A_WIDTH), lambda g, pt: (g // n_chunks, 0, 0)),
                   pl.BlockSpec((1, 1, past + SCORE_PAD), lambda g, pt: (g // n_chunks, 0, 0))],
        scratch_shapes=[pltpu.VMEM((2, ck, KV_LORA), F32), pltpu.VMEM((2, A_ROPE, ck), F32),
                        pltpu.VMEM((2, IDX_DIM, ck), F32), pltpu.SemaphoreType.DMA((2, 3)),
                        pltpu.VMEM((A_HEADS, 1), F32), pltpu.VMEM((A_HEADS, 1), F32),
                        pltpu.VMEM((A_HEADS, KV_LORA), F32)],
    )
    return pl.pallas_call(
        functools.partial(_sample_pass1_kernel, n_chunks=n_chunks, pages_per_chunk=pages_per_chunk,
                          page=page, past=past),
        grid_spec=grid_spec,
        out_shape=[jax.ShapeDtypeStruct((nq, 1, A_WIDTH), F32),
                   jax.ShapeDtypeStruct((nq, 1, past + SCORE_PAD), F32)],
        compiler_params=_params(("arbitrary",)),
    )(page_table, qlat, qpe, qidx, wi, ckvn, kpen, kidxn, gate_a, w_uv, pool_ckv, pool_kpe, pool_kidx)


def _topk_thresh_kernel(score_ref, thr_ref, *, k_top):
    key = _sort_key(score_ref[...])

    def count_ge(cand):
        ge = (key >= cand).astype(F32)
        return jnp.sum(jnp.sum(ge, axis=2, keepdims=True), axis=1, keepdims=True)

    thr = _kth_largest_key(count_ge, (key.shape[0], 1, 1), k_top)
    thr_ref[...] = jnp.broadcast_to(thr, thr_ref.shape)


def _topk_thresh(score3, k_top):
    nq = score3.shape[0]
    return pl.pallas_call(
        functools.partial(_topk_thresh_kernel, k_top=k_top),
        in_specs=[pl.BlockSpec(memory_space=pltpu.VMEM)],
        out_specs=pl.BlockSpec(memory_space=pltpu.VMEM),
        out_shape=jax.ShapeDtypeStruct((nq, 1, LANES), I32),
        compiler_params=_params(),
    )(score3)


def _sample_pass2_kernel(pt_ref, cfar_ref, qb_ref, kbn_ref, vbn_ref, gb_ref, score2_ref, snew_ref, thr_ref,
                         tail_ref, bnew_ref, pool_k, pool_v, out_ref,
                         buf_k, buf_v, sems, m_scr, l_scr, acc_scr,
                         *, n_chunks, pages_per_chunk, page):
    g = pl.program_id(0)
    n_steps = pl.num_programs(0)
    c = g % n_chunks
    slot = g % 2
    pools = (pool_k, pool_v)
    bufs = (buf_k, buf_v)
    ck = pages_per_chunk * page

    def copies(step, slot_):
        return _page_copies(pt_ref, pools, bufs, (False, False), sems, step // n_chunks,
                            step % n_chunks, slot_, pages_per_chunk)

    @pl.when(g == 0)
    def _():
        for cp in copies(g, slot):
            cp.start()

    @pl.when(g + 1 < n_steps)
    def _():
        for cp in copies(g + 1, 1 - slot):
            cp.start()

    qb = qb_ref[0].astype(F32)
    thr = thr_ref[0][:, :1]
    cfar = cfar_ref[...]
    kv_of_row = (lax.broadcasted_iota(I32, (B_HEADS, 1), 0) >= B_GROUP).astype(I32)
    head0 = kv_of_row == 0
    lo, hi = slice(0, B_HD), slice(B_HD, 2 * B_HD)

    @pl.when(c == 0)
    def _():
        key_new = _sort_key(snew_ref[0, :, :LANES])[:, :1]
        keep = key_new >= thr
        kbn = kbn_ref[0]
        vbn = vbn_ref[0]
        s_new = jnp.where(head0, jnp.sum(qb * kbn[:, lo], axis=-1, keepdims=True),
                          jnp.sum(qb * kbn[:, hi], axis=-1, keepdims=True))
        s_new = s_new * B_SCALE + bnew_ref[...]
        v_new = jnp.where(head0, jnp.broadcast_to(vbn[:, lo], (B_HEADS, B_HD)),
                          jnp.broadcast_to(vbn[:, hi], (B_HEADS, B_HD)))
        m_scr[...] = jnp.where(keep, s_new, NEG)
        l_scr[...] = jnp.where(keep, jnp.ones_like(s_new), 0.0)
        acc_scr[...] = jnp.where(keep, v_new, 0.0)

    for cp in copies(g, slot):
        cp.wait()

    w2 = 2 * ck
    sel = _sort_key(score2_ref[0]) >= thr
    lane = lax.broadcasted_iota(I32, (B_HEADS, w2), 1)
    valid = jnp.broadcast_to(sel, (B_HEADS, w2)) & ((lane & 1) == kv_of_row)
    in_tail = (lane >= w2 - 2 * page) & (c == n_chunks - 1)
    if ck > page:
        tail = jnp.concatenate([jnp.zeros((B_HEADS, w2 - 2 * page), F32), tail_ref[...]], axis=1)
    else:
        tail = tail_ref[...]
    bias = jnp.where(in_tail, tail, cfar)
    s = _dot_nt(qb, buf_k[slot]) * B_SCALE + bias
    s = jnp.where(valid, s, NEG)
    m = m_scr[...]
    m_new = jnp.maximum(m, jnp.max(s, axis=-1, keepdims=True))
    p = jnp.where(valid, jnp.exp(s - m_new), 0.0)
    alpha = jnp.exp(m - m_new)
    m_scr[...] = m_new
    l_scr[...] = alpha * l_scr[...] + jnp.sum(p, axis=-1, keepdims=True)
    acc_scr[...] = alpha * acc_scr[...] + _dot(p, buf_v[slot])

    @pl.when(c == n_chunks - 1)
    def _():
        o = acc_scr[...] / l_scr[...]
        for hh in range(B_HEADS):
            cols = slice(hh * B_HD, (hh + 1) * B_HD)
            out_ref[0, :, cols] = o[hh:hh + 1] * _silu(gb_ref[0, :, cols])


def _sample_pass2(page_table, cfar, qb, kbn, vbn, gate_b, score2, score, thr, tail2, bias_new,
                  pool_k2, pool_v2, pages_per_chunk):
    nq, n_pages = page_table.shape
    page = pool_k2.shape[2] // B_KV_HEADS
    past = n_pages * page
    n_chunks = n_pages // pages_per_chunk
    ck = pages_per_chunk * page
    assert past % SCORE_PAD == 0
    per_q = lambda a: pl.BlockSpec((1,) + a.shape[1:], lambda g, pt: (g // n_chunks, 0, 0))
    const2 = lambda a: pl.BlockSpec(a.shape, lambda g, pt: (0, 0))
    any_spec = pl.BlockSpec(memory_space=pl.ANY)
    grid_spec = pltpu.PrefetchScalarGridSpec(
        num_scalar_prefetch=1,
        grid=(nq * n_chunks,),
        in_specs=[const2(cfar), per_q(qb), per_q(kbn), per_q(vbn), per_q(gate_b),
                  pl.BlockSpec((1, 1, 2 * ck), lambda g, pt: (g // n_chunks, 0, g % n_chunks)),
                  pl.BlockSpec((1, 1, SCORE_PAD), lambda g, pt: (g // n_chunks, 0, past // SCORE_PAD)),
                  per_q(thr), const2(tail2), const2(bias_new), any_spec, any_spec],
        out_specs=pl.BlockSpec((1, 1, B_WIDTH), lambda g, pt: (g // n_chunks, 0, 0)),
        scratch_shapes=[pltpu.VMEM((2, 2 * ck, B_HD), F32), pltpu.VMEM((2, 2 * ck, B_HD), F32),
                        pltpu.SemaphoreType.DMA((2, 2)),
                        pltpu.VMEM((B_HEADS, 1), F32), pltpu.VMEM((B_HEADS, 1), F32),
                        pltpu.VMEM((B_HEADS, B_HD), F32)],
    )
    return pl.pallas_call(
        functools.partial(_sample_pass2_kernel, n_chunks=n_chunks, pages_per_chunk=pages_per_chunk,
                          page=page),
        grid_spec=grid_spec,
        out_shape=jax.ShapeDtypeStruct((nq, 1, B_WIDTH), F32),
        compiler_params=_params(("arbitrary",)),
    )(page_table, cfar, qb, kbn, vbn, gate_b, score2, score, thr, tail2, bias_new, pool_k2, pool_v2)


def _prep_w_in(w_in):
    offs = np.concatenate([[0], np.cumsum(IN_SIZES)])
    part = lambda k: w_in[:, offs[k]:offs[k + 1]]
    c_q, c_kv, k_pe, gate_a, q_b, k_b, v_b, gate_b, q_idx, k_idx, w_idx = [part(k) for k in range(11)]
    pad = jnp.zeros((w_in.shape[0], LANES - IDX_HEADS), w_in.dtype)
    cols = [c_q, c_kv, gate_a, q_b, k_b, v_b, gate_b, q_idx, k_pe, k_idx, w_idx, pad]
    return jnp.concatenate(cols, axis=1).astype(BF16)


def _prep_w_uq(w_uq):
    w = w_uq.reshape(w_uq.shape[0], A_HEADS, A_NOPE + A_ROPE)
    nope = w[:, :, :A_NOPE].reshape(w_uq.shape[0], A_HEADS * A_NOPE)
    rope = w[:, :, A_NOPE:].reshape(w_uq.shape[0], A_HEADS * A_ROPE)
    return jnp.concatenate([nope, rope], axis=1).astype(BF16)


def kernel(x_prompt, x_sample, p_prompt, p_sample, cache_ckv, cache_kpe, cache_k, cache_v, cache_kidx,
           page_table, rel_table, g_attn, w_in, g_q, w_uq, g_kv, w_uk, w_uv, w_out, g_ple, w_ple_gate,
           w_ple_proj, g_final):
    nb, seq, d = x_prompt.shape
    nq, nt, _ = x_sample.shape
    depth = w_in.shape[0]
    assert depth == 1 and nt == 1
    n_pages = page_table.shape[1]
    page = cache_ckv.shape[2]
    past = n_pages * page
    tq = min(256, seq)
    tm_s = min(128, nq)
    pages_per_chunk = min(32, n_pages)
    assert seq % tq == 0 and nq % tm_s == 0 and n_pages % pages_per_chunk == 0
    assert tq + 1 > REL_MAX_DIST

    row2 = lambda v: v.reshape(1, -1)
    w_in_p = _prep_w_in(w_in[0])
    w_uq_p = _prep_w_uq(w_uq[0])
    w_uk_b = w_uk[0].astype(BF16)
    w_uv_b = w_uv[0].astype(BF16)
    w_out_b = w_out[0].astype(BF16)
    w_gate_b = w_ple_gate[0].astype(BF16)
    w_proj_b = w_ple_proj[0].astype(BF16)
    cfar = rel_table[REL_BUCKETS - 1]

    tabs_p = _rope_tables(jnp.arange(seq, dtype=I32))
    xp = x_prompt.reshape(nb * seq, d)
    (qcat, qi2, _, _, wi, qb, ga, gb, ckv, kpe, kb, vb, kidx, kcat, ii, kbb, vbb) = _proj_in(
        xp, tabs_p, seq // tq, row2(g_attn[0]), w_in_p, row2(g_q[0]), w_uq_p, row2(g_kv[0]), w_uk_b, tq)
    b3 = lambda a: a.reshape((nb, seq) + a.shape[1:])
    b4 = lambda a: a.reshape((a.shape[0], nb, seq) + a.shape[2:])
    ma = _mla_prompt(b4(qcat), b3(kcat), b3(ga), w_uv_b, tq)
    rel = np.arange(tq)[:, None] + tq - np.arange(2 * tq)[None, :]
    bias = _bias_tiles(rel_table, rel)
    k_top = min(IDX_TOPK, seq // 4)
    mb = _dsa_prompt(cfar, b4(qi2), b3(wi), b3(ii), b3(qb), b3(kbb), b3(vbb), b3(gb), bias, tq, k_top)
    y_prompt = _out_proj(xp, ma.reshape(nb * seq, -1), mb.reshape(nb * seq, -1),
                         p_prompt[0].reshape(nb * seq, -1), w_out_b, row2(g_ple[0]), w_gate_b, w_proj_b,
                         row2(g_final), tq).reshape(nb, seq, d)

    tabs_s = _rope_tables(jnp.full((tm_s,), past, I32))
    xs = x_sample.reshape(nq, d)
    (qcat_s, _, qpe_s, qidx_s, wi_s, qb_s, ga_s, gb_s, ckv_s, kpe_s, kb_s, vb_s, kidx_s, _, _, _, _) = _proj_in(
        xs, tabs_s, 1, row2(g_attn[0]), w_in_p, row2(g_q[0]), w_uq_p, row2(g_kv[0]), w_uk_b, tm_s)
    qlat_s = jnp.swapaxes(qcat_s[:, :, :KV_LORA], 0, 1)
    ma_s, score = _sample_pass1(
        page_table, qlat_s, qpe_s.reshape(nq, A_HEADS, A_ROPE), qidx_s.reshape(nq, IDX_HEADS, IDX_DIM),
        wi_s[:, :IDX_HEADS].reshape(nq, IDX_HEADS, 1), ckv_s.reshape(nq, 1, -1), kpe_s.reshape(nq, 1, -1),
        kidx_s.reshape(nq, 1, -1), ga_s.reshape(nq, 1, -1), w_uv_b, cache_ckv,
        jnp.swapaxes(cache_kpe, 2, 3), jnp.swapaxes(cache_kidx, 2, 3), pages_per_chunk)
    k_top_s = min(IDX_TOPK, (past + nt) // 4)
    thr = _topk_thresh(score.reshape(nq, n_pages + SCORE_PAD // LANES, LANES), k_top_s)
    rel_tail = np.broadcast_to(np.concatenate(
        [np.repeat(page - np.arange(page), 2), np.zeros(LANES, np.int64)])[None, :], (8, 2 * page + LANES))
    tail = _bias_tiles(rel_table, rel_tail)[:, 0, :]
    rows2 = lambda pool: pool.reshape(pool.shape[0], pool.shape[1], B_KV_HEADS * page, B_HD)
    mb_s = _sample_pass2(page_table, cfar.reshape(B_HEADS, 1), qb_s.reshape(nq, B_HEADS, B_HD),
                         kb_s.reshape(nq, 1, -1), vb_s.reshape(nq, 1, -1), gb_s.reshape(nq, 1, -1),
                         jnp.repeat(score[:, :, :past], 2, axis=2), score, thr,
                         tail[:, :2 * page], tail[:, 2 * page:2 * page + 1],
                         rows2(cache_k), rows2(cache_v), pages_per_chunk)
    y_sample = _out_proj(xs, ma_s.reshape(nq, -1).astype(BF16), mb_s.reshape(nq, -1).astype(BF16),
                         p_sample[0].reshape(nq, -1), w_out_b, row2(g_ple[0]), w_gate_b, w_proj_b,
                         row2(g_final), tm_s).reshape(nq, nt, d)

    return (y_prompt, y_sample,
            ckv.reshape(1, nb, seq, -1), kpe.reshape(1, nb, seq, -1),
            kb.reshape(1, nb, seq, B_KV_HEADS, B_HD), vb.reshape(1, nb, seq, B_KV_HEADS, B_HD),
            kidx.reshape(1, nb, seq, -1),
            ckv_s.reshape(1, nq, nt, -1), kpe_s.reshape(1, nq, nt, -1),
            kb_s.reshape(1, nq, nt, B_KV_HEADS, B_HD), vb_s.reshape(1, nq, nt, B_KV_HEADS, B_HD),
            kidx_s.reshape(1, nq, nt, -1))
```

```python
import functools
import math

import numpy as np
import jax
import jax.numpy as jnp
from jax import lax
from jax.experimental import pallas as pl
from jax.experimental.pallas import tpu as pltpu

F32 = jnp.float32
BF16 = jnp.bfloat16
I32 = jnp.int32

A_HEADS = 8
A_NOPE = 128
A_ROPE = 64
A_V = 128
A_WIDTH = A_HEADS * A_V
Q_LORA = 512
KV_LORA = 256
B_HEADS = 8
B_KV_HEADS = 2
B_HD = 128
B_GROUP = B_HEADS // B_KV_HEADS
B_WIDTH = B_HEADS * B_HD
B_KVW = B_KV_HEADS * B_HD
IDX_HEADS = 16
IDX_DIM = 64
IDX_ROPE = 32
IDX_TOPK = 256
REL_BUCKETS = 32
REL_MAX_DIST = 128
ROPE_THETA = 10000.0
EPS = 1e-6
MLA_SCALE = (A_NOPE + A_ROPE) ** -0.5
B_SCALE = B_HD ** -0.5
IDX_W_SCALE = (IDX_HEADS ** -0.5) * (IDX_DIM ** -0.5)
IN_SIZES = (Q_LORA, KV_LORA, A_ROPE, A_WIDTH, B_WIDTH, B_KVW, B_KVW, B_WIDTH,
            IDX_HEADS * IDX_DIM, IDX_DIM, IDX_HEADS)

LANES = 128
NEG = -1e30
VMEM_LIMIT = 56 * 1024 * 1024
INT_MIN = -2 ** 31

C_Q = 0
C_KV = C_Q + Q_LORA
C_GA = C_KV + KV_LORA
C_QB = C_GA + A_WIDTH
C_KB = C_QB + B_WIDTH
C_VB = C_KB + B_KVW
C_GB = C_VB + B_KVW
C_QI = C_GB + B_WIDTH
C_SLAB = C_QI + IDX_HEADS * IDX_DIM
C_WI = C_SLAB + LANES
NP_IN = C_WI + LANES
QCAT = KV_LORA + LANES
N_TAB = 11
SCORE_PAD = 8 * LANES


def _dot(a, b):
    return jnp.dot(a, b, preferred_element_type=F32)


def _dot_nt(a, b):
    return lax.dot_general(a, b, (((1,), (1,)), ((), ())), preferred_element_type=F32)


def _silu(x):
    return x / (1.0 + jnp.exp(-x))


def _params(sem=None):
    return pltpu.CompilerParams(dimension_semantics=sem, vmem_limit_bytes=VMEM_LIMIT)


def _resident(shape):
    nd = len(shape)
    return pl.BlockSpec(shape, lambda *_: (0,) * nd, pipeline_mode=pl.Buffered(1))


def _rope_tables(pos):
    def cs(half):
        inv = ROPE_THETA ** (-jnp.arange(half, dtype=F32) / half)
        ang = pos.astype(F32)[:, None] * inv[None, :]
        return jnp.cos(ang), jnp.sin(ang)
    n = pos.shape[0]
    c32, s32 = cs(A_ROPE // 2)
    c16, s16 = cs(IDX_ROPE // 2)
    z16 = jnp.zeros((n, 16), F32)
    z32 = jnp.zeros((n, 32), F32)
    z64 = jnp.zeros((n, 64), F32)
    o32 = jnp.ones((n, 32), F32)
    cat = lambda *a: jnp.concatenate(a, axis=1)
    a64, bm64, bp64 = cat(c32, c32), cat(-s32, z32), cat(z32, s32)
    a16, cm16, cp16 = cat(c16, c16, o32), cat(-s16, z16, z32), cat(z16, s16, z32)
    tabs = [cat(a64, a64), cat(bm64, bm64), cat(bp64, bp64),
            cat(a16, a16), cat(cm16, cm16), cat(cp16, cp16),
            cat(a64, a16), cat(bm64, z64), cat(bp64, z64), cat(z64, cm16), cat(z64, cp16)]
    return jnp.concatenate(tabs, axis=1)


def _proj_in_kernel(x_ref, gattn_ref, win_ref, gq_ref, wuq_ref, gkv_ref, wuk_ref, tab_ref,
                    qcat_ref, qi2_ref, qpe_ref, qidx_ref, wi_ref, qb_ref, ga_ref, gb_ref,
                    ckv_ref, kpe_ref, kb_ref, vb_ref, kidx_ref, kcat_ref, ii_ref, kbb_ref, vbb_ref,
                    h_scr):
    x = x_ref[...]
    ms = jnp.mean(x * x, axis=-1, keepdims=True)
    h_scr[...] = (x * lax.rsqrt(ms + EPS) * gattn_ref[...]).astype(BF16)

    def zcols(c0, width):
        return _dot(h_scr[...], win_ref[:, c0:c0 + width])

    def tab(k):
        return tab_ref[:, k * LANES:(k + 1) * LANES]

    tm = x.shape[0]
    lane = lax.broadcasted_iota(I32, (tm, LANES), 1)
    lo = lane < 64

    def rope(xg, a, bm, bp, shift):
        return (xg * tab(a) + pltpu.roll(xg, LANES - shift, 1) * tab(bm)
                + pltpu.roll(xg, shift, 1) * tab(bp))

    cq = zcols(C_Q, Q_LORA)
    cq = cq * lax.rsqrt(jnp.mean(cq * cq, axis=-1, keepdims=True) + EPS) * gq_ref[...]
    q = _dot(cq.astype(BF16), wuq_ref[...])
    n_nope = A_HEADS * A_NOPE
    for g in range(A_HEADS // 2):
        xg = q[:, n_nope + g * LANES:n_nope + (g + 1) * LANES]
        rg = rope(xg, 0, 1, 2, A_ROPE // 2)
        qpe_ref[:, g * LANES:(g + 1) * LANES] = rg.astype(BF16)
        qcat_ref[2 * g, :, KV_LORA:] = jnp.where(lo, rg, 0.0).astype(BF16)
        qcat_ref[2 * g + 1, :, KV_LORA:] = jnp.where(lo, 0.0, rg).astype(BF16)
    for hh in range(A_HEADS):
        qn = q[:, hh * A_NOPE:(hh + 1) * A_NOPE].astype(BF16)
        qcat_ref[hh, :, :KV_LORA] = _dot(qn, wuk_ref[hh]).astype(BF16)

    ckv = zcols(C_KV, KV_LORA)
    ckv = ckv * lax.rsqrt(jnp.mean(ckv * ckv, axis=-1, keepdims=True) + EPS) * gkv_ref[...]
    ckv_ref[...] = ckv
    kcat_ref[:, :KV_LORA] = ckv.astype(BF16)

    slab = zcols(C_SLAB, LANES)
    slab = (slab * tab(6) + pltpu.roll(slab, LANES - 32, 1) * tab(7) + pltpu.roll(slab, 32, 1) * tab(8)
            + pltpu.roll(slab, LANES - 16, 1) * tab(9) + pltpu.roll(slab, 16, 1) * tab(10))
    swapped = pltpu.roll(slab, 64, 1)
    kpe_ref[...] = slab[:, :A_ROPE]
    kidx_ref[...] = swapped[:, :IDX_DIM]
    kcat_ref[:, KV_LORA:] = jnp.where(lo, slab, swapped).astype(BF16)
    ii_ref[...] = jnp.where(lo, swapped, slab).astype(BF16)

    kb = zcols(C_KB, B_KVW)
    kb_ref[...] = kb
    kbb_ref[...] = kb.astype(BF16)
    vb = zcols(C_VB, B_KVW)
    vb_ref[...] = vb
    vbb_ref[...] = vb.astype(BF16)
    for c in range(B_WIDTH // 512):
        qb_ref[:, c * 512:(c + 1) * 512] = zcols(C_QB + c * 512, 512).astype(BF16)
        ga_ref[:, c * 512:(c + 1) * 512] = zcols(C_GA + c * 512, 512)
        gb_ref[:, c * 512:(c + 1) * 512] = zcols(C_GB + c * 512, 512)

    for c in range(IDX_HEADS * IDX_DIM // 512):
        qi = zcols(C_QI + c * 512, 512)
        for g in range(4):
            rg = rope(qi[:, g * LANES:(g + 1) * LANES], 3, 4, 5, IDX_ROPE // 2)
            col = c * 512 + g * LANES
            qidx_ref[:, col:col + LANES] = rg.astype(BF16)
            hh = col // IDX_DIM
            qi2_ref[hh] = jnp.where(lo, rg, 0.0).astype(BF16)
            qi2_ref[hh + 1] = jnp.where(lo, 0.0, rg).astype(BF16)
    wi_ref[...] = zcols(C_WI, LANES) * IDX_W_SCALE


def _proj_in(x, tabs, tab_blocks, g_attn, w_in_p, g_q, w_uq_p, g_kv, w_uk, tm):
    n, d = x.shape
    grid = (n // tm,)
    row = lambda w: pl.BlockSpec((tm, w), lambda i: (i, 0))
    head = lambda nh, w: pl.BlockSpec((nh, tm, w), lambda i: (0, i, 0))
    out_shape = [
        jax.ShapeDtypeStruct((A_HEADS, n, QCAT), BF16),
        jax.ShapeDtypeStruct((IDX_HEADS, n, LANES), BF16),
        jax.ShapeDtypeStruct((n, A_HEADS * A_ROPE), BF16),
        jax.ShapeDtypeStruct((n, IDX_HEADS * IDX_DIM), BF16),
        jax.ShapeDtypeStruct((n, LANES), F32),
        jax.ShapeDtypeStruct((n, B_WIDTH), BF16),
        jax.ShapeDtypeStruct((n, A_WIDTH), F32),
        jax.ShapeDtypeStruct((n, B_WIDTH), F32),
        jax.ShapeDtypeStruct((n, KV_LORA), F32),
        jax.ShapeDtypeStruct((n, A_ROPE), F32),
        jax.ShapeDtypeStruct((n, B_KVW), F32),
        jax.ShapeDtypeStruct((n, B_KVW), F32),
        jax.ShapeDtypeStruct((n, IDX_DIM), F32),
        jax.ShapeDtypeStruct((n, QCAT), BF16),
        jax.ShapeDtypeStruct((n, LANES), BF16),
        jax.ShapeDtypeStruct((n, B_KVW), BF16),
        jax.ShapeDtypeStruct((n, B_KVW), BF16),
    ]
    out_specs = [head(A_HEADS, QCAT), head(IDX_HEADS, LANES), row(A_HEADS * A_ROPE),
                 row(IDX_HEADS * IDX_DIM), row(LANES), row(B_WIDTH), row(A_WIDTH), row(B_WIDTH),
                 row(KV_LORA), row(A_ROPE), row(B_KVW), row(B_KVW), row(IDX_DIM),
                 row(QCAT), row(LANES), row(B_KVW), row(B_KVW)]
    in_specs = [row(d), _resident(g_attn.shape), _resident(w_in_p.shape), _resident(g_q.shape),
                _resident(w_uq_p.shape), _resident(g_kv.shape), _resident(w_uk.shape),
                pl.BlockSpec((tm, N_TAB * LANES), lambda i: (i % tab_blocks, 0))]
    return pl.pallas_call(
        _proj_in_kernel, grid=grid, in_specs=in_specs, out_specs=out_specs, out_shape=out_shape,
        scratch_shapes=[pltpu.VMEM((tm, d), BF16)],
        compiler_params=_params(("parallel",)),
    )(x, g_attn, w_in_p, g_q, w_uq_p, g_kv, w_uk, tabs)


def _flash_update(carry, s, valid, v):
    m, l, acc = carry
    if valid is not None:
        s = jnp.where(valid, s, NEG)
    m_new = jnp.maximum(m, jnp.max(s, axis=-1, keepdims=True))
    p = jnp.exp(s - m_new)
    if valid is not None:
        p = jnp.where(valid, p, 0.0)
    alpha = jnp.exp(m - m_new)
    l = alpha * l + jnp.sum(p, axis=-1, keepdims=True)
    acc = alpha * acc + _dot(p.astype(v.dtype), v)
    return m_new, l, acc


def _flash_init(m, d):
    return (jnp.full((m, 1), NEG, F32), jnp.zeros((m, 1), F32), jnp.zeros((m, d), F32))


def _flash_reset(m_scr, l_scr, acc_scr):
    m_scr[...] = jnp.full(m_scr.shape, NEG, F32)
    l_scr[...] = jnp.zeros(l_scr.shape, F32)
    acc_scr[...] = jnp.zeros(acc_scr.shape, F32)


def _flash_step(m_scr, l_scr, acc_scr, hh, s, valid, v):
    m, l, acc = _flash_update((m_scr[hh], l_scr[hh], acc_scr[hh]), s, valid, v)
    m_scr[hh] = m
    l_scr[hh] = l
    acc_scr[hh] = acc


def _mla_prompt_kernel(qcat_ref, kcat_ref, ga_ref, wuv_ref, out_ref, m_scr, l_scr, acc_scr, *, tq):
    i = pl.program_id(1)
    row = lax.broadcasted_iota(I32, (tq, tq), 0)
    col = lax.broadcasted_iota(I32, (tq, tq), 1)
    causal = col <= row
    _flash_reset(m_scr, l_scr, acc_scr)

    def chunk(j, valid):
        kc = kcat_ref[0, pl.ds(pl.multiple_of(j * tq, tq), tq), :]
        for hh in range(A_HEADS):
            s = _dot_nt(qcat_ref[hh, 0], kc) * MLA_SCALE
            _flash_step(m_scr, l_scr, acc_scr, hh, s, valid, kc[:, :KV_LORA])

    def far(j, carry):
        chunk(j, None)
        return carry

    lax.fori_loop(0, i, far, 0)
    chunk(i, causal)
    for hh in range(A_HEADS):
        lat = (acc_scr[hh] / l_scr[hh]).astype(BF16)
        o = _dot(lat, wuv_ref[hh])
        cols = slice(hh * A_V, (hh + 1) * A_V)
        out_ref[0, :, cols] = (o * _silu(ga_ref[0, :, cols])).astype(BF16)


def _mla_prompt(qcat, kcat, gate_a, w_uv, tq):
    nb, seq = kcat.shape[0], kcat.shape[1]
    return pl.pallas_call(
        functools.partial(_mla_prompt_kernel, tq=tq),
        grid=(nb, seq // tq),
        in_specs=[pl.BlockSpec((A_HEADS, 1, tq, QCAT), lambda b, i: (0, b, i, 0)),
                  pl.BlockSpec((1, seq, QCAT), lambda b, i: (b, 0, 0)),
                  pl.BlockSpec((1, tq, A_WIDTH), lambda b, i: (b, i, 0)),
                  _resident(w_uv.shape)],
        out_specs=pl.BlockSpec((1, tq, A_WIDTH), lambda b, i: (b, i, 0)),
        out_shape=jax.ShapeDtypeStruct((nb, seq, A_WIDTH), BF16),
        scratch_shapes=[pltpu.VMEM((A_HEADS, tq, 1), F32), pltpu.VMEM((A_HEADS, tq, 1), F32),
                        pltpu.VMEM((A_HEADS, tq, KV_LORA), F32)],
        compiler_params=_params(("parallel", "parallel")),
    )(qcat, kcat, gate_a, w_uv)


def _sort_key(score):
    b = pltpu.bitcast(score + 0.0, I32)
    return jnp.where(b < 0, b ^ 0x7FFFFFFF, b)


def _kth_largest_key(count_ge, shape, k):
    def body(it, lo):
        bit = 31 - it
        cand = lo + lax.shift_left(jnp.int32(1), bit)
        return jnp.where(count_ge(cand) >= k, cand, lo)
    return lax.fori_loop(0, 32, body, jnp.full(shape, INT_MIN, I32))


def _dsa_prompt_kernel(cfar_ref, qi2_ref, wi_ref, ii_ref, qb_ref, kbb_ref, vbb_ref, gb_ref, bias_ref,
                       out_ref, key_scr, thr_scr, m_scr, l_scr, acc_scr, *, tq, seq, k_top, cw):
    i = pl.program_id(1)
    t0 = i * tq
    w = wi_ref[0]
    qrow = t0 + lax.broadcasted_iota(I32, (tq, cw), 0)
    kcol = lax.broadcasted_iota(I32, (tq, cw), 1)

    for c in range(seq // cw):
        @pl.when(c * cw < t0 + tq)
        def _():
            kc = ii_ref[0, c * cw:(c + 1) * cw, :]
            acc = jnp.zeros((tq, cw), F32)
            for hh in range(IDX_HEADS):
                s = _dot_nt(qi2_ref[hh, 0], kc)
                acc = acc + w[:, hh:hh + 1] * jnp.maximum(s, 0.0)
            acc = jnp.where(kcol + c * cw <= qrow, acc, -jnp.inf)
            key = _sort_key(acc)
            for u in range(cw // tq):
                key_scr[c * (cw // tq) + u] = key[:, u * tq:(u + 1) * tq]

    for n in range(1, seq // tq + 1):
        @pl.when(i == n - 1)
        def _():
            def count_ge(cand):
                part = jnp.zeros((tq, LANES), F32)
                for j in range(n):
                    ge = (key_scr[j] >= cand).astype(F32)
                    for u in range(tq // LANES):
                        part = part + ge[:, u * LANES:(u + 1) * LANES]
                return jnp.sum(part, axis=1, keepdims=True)

            thr_scr[...] = _kth_largest_key(count_ge, (tq, 1), k_top)

    thr = thr_scr[...]

    row = lax.broadcasted_iota(I32, (tq, tq), 0)
    col = lax.broadcasted_iota(I32, (tq, tq), 1)
    causal = col <= row
    _flash_reset(m_scr, l_scr, acc_scr)

    def chunk(j, bias_of, extra):
        off = pl.multiple_of(j * tq, tq)
        sel = key_scr[j] >= thr
        if extra is not None:
            sel = sel & extra
        for hh in range(B_HEADS):
            kv_cols = slice((hh // B_GROUP) * B_HD, (hh // B_GROUP + 1) * B_HD)
            qh = qb_ref[0, :, hh * B_HD:(hh + 1) * B_HD]
            s = _dot_nt(qh, kbb_ref[0, pl.ds(off, tq), kv_cols]) * B_SCALE + bias_of(hh)
            _flash_step(m_scr, l_scr, acc_scr, hh, s, sel, vbb_ref[0, pl.ds(off, tq), kv_cols])

    def far(j, carry):
        chunk(j, lambda hh: cfar_ref[hh], None)
        return carry

    lax.fori_loop(0, jnp.maximum(i - 1, 0), far, 0)

    @pl.when(i > 0)
    def _():
        chunk(i - 1, lambda hh: bias_ref[hh, :, :tq], None)

    chunk(i, lambda hh: bias_ref[hh, :, tq:], causal)
    for hh in range(B_HEADS):
        cols = slice(hh * B_HD, (hh + 1) * B_HD)
        out_ref[0, :, cols] = ((acc_scr[hh] / l_scr[hh]) * _silu(gb_ref[0, :, cols])).astype(BF16)


def _dsa_prompt(cfar, qi2, wi, ii, qb, kbb, vbb, gate_b, bias, tq, k_top):
    nb, seq = ii.shape[0], ii.shape[1]
    cw = min(512, seq)
    assert cw % tq == 0 and seq % cw == 0 and k_top <= tq
    full = lambda w: pl.BlockSpec((1, seq, w), lambda b, i: (b, 0, 0))
    tile = lambda w: pl.BlockSpec((1, tq, w), lambda b, i: (b, i, 0))
    return pl.pallas_call(
        functools.partial(_dsa_prompt_kernel, tq=tq, seq=seq, k_top=k_top, cw=cw),
        grid=(nb, seq // tq),
        in_specs=[pl.BlockSpec(memory_space=pltpu.SMEM),
                  pl.BlockSpec((IDX_HEADS, 1, tq, LANES), lambda b, i: (0, b, i, 0)),
                  tile(LANES), full(LANES), tile(B_WIDTH), full(B_KVW), full(B_KVW), tile(B_WIDTH),
                  _resident(bias.shape)],
        out_specs=tile(B_WIDTH),
        out_shape=jax.ShapeDtypeStruct((nb, seq, B_WIDTH), BF16),
        scratch_shapes=[pltpu.VMEM((seq // tq, tq, tq), I32), pltpu.VMEM((tq, 1), I32),
                        pltpu.VMEM((B_HEADS, tq, 1), F32), pltpu.VMEM((B_HEADS, tq, 1), F32),
                        pltpu.VMEM((B_HEADS, tq, B_HD), F32)],
        compiler_params=_params(("parallel", "parallel")),
    )(cfar, qi2, wi, ii, qb, kbb, vbb, gate_b, bias)


def _bias_kernel(table_ref, bucket_ref, out_ref):
    bucket = bucket_ref[...]
    for hh in range(B_HEADS):
        acc = jnp.zeros(bucket.shape, F32)
        for b in range(REL_BUCKETS):
            acc = jnp.where(bucket == b, table_ref[b, hh], acc)
        out_ref[hh] = acc


def _t5_bucket_np(rel):
    n = np.maximum(rel, 0)
    exact = REL_BUCKETS // 2
    nf = np.maximum(n, 1).astype(np.float32)
    large = exact + (np.log(nf / exact) / math.log(REL_MAX_DIST / exact)
                     * (REL_BUCKETS - exact)).astype(np.int32)
    return np.where(n < exact, n, np.minimum(large, REL_BUCKETS - 1)).astype(np.int32)


def _bias_tiles(rel_table, rel):
    bucket = jnp.asarray(_t5_bucket_np(rel))
    return pl.pallas_call(
        _bias_kernel,
        in_specs=[pl.BlockSpec(memory_space=pltpu.SMEM), pl.BlockSpec(memory_space=pltpu.VMEM)],
        out_specs=pl.BlockSpec(memory_space=pltpu.VMEM),
        out_shape=jax.ShapeDtypeStruct((B_HEADS,) + rel.shape, F32),
    )(rel_table, bucket)


def _out_proj_kernel(x_ref, ma_ref, mb_ref, p_ref, wout_ref, gple_ref, wgate_ref, wproj_ref, gfin_ref,
                     y_ref, x1_scr, hn_scr, *, cw):
    d = x_ref.shape[1]
    nc = d // cw
    half = ma_ref.shape[1]
    ss = jnp.zeros((x_ref.shape[0], 1), F32)
    for c in range(nc):
        cols = slice(c * cw, (c + 1) * cw)
        x1 = (x_ref[:, cols] + _dot(ma_ref[...], wout_ref[:half, cols])
              + _dot(mb_ref[...], wout_ref[half:, cols]))
        x1_scr[:, cols] = x1
        ss = ss + jnp.sum(x1 * x1, axis=-1, keepdims=True)
    rs = lax.rsqrt(ss / d + EPS)
    for c in range(nc):
        cols = slice(c * cw, (c + 1) * cw)
        hn_scr[:, cols] = (x1_scr[:, cols] * rs * gple_ref[:, cols]).astype(BF16)
    pb = p_ref[...].astype(BF16)
    ss = jnp.zeros((x_ref.shape[0], 1), F32)
    for c in range(nc):
        cols = slice(c * cw, (c + 1) * cw)
        gt = 1.0 / (1.0 + jnp.exp(-_dot(hn_scr[...], wgate_ref[:, cols])))
        x2 = x1_scr[:, cols] + gt * _dot(pb, wproj_ref[:, cols])
        x1_scr[:, cols] = x2
        ss = ss + jnp.sum(x2 * x2, axis=-1, keepdims=True)
    rs = lax.rsqrt(ss / d + EPS)
    for c in range(nc):
        cols = slice(c * cw, (c + 1) * cw)
        y_ref[:, cols] = x1_scr[:, cols] * rs * gfin_ref[:, cols]


def _out_proj(x, ma, mb, p, w_out, g_ple, w_gate, w_proj, g_final, tm):
    n, d = x.shape
    row = lambda w: pl.BlockSpec((tm, w), lambda i: (i, 0))
    return pl.pallas_call(
        functools.partial(_out_proj_kernel, cw=min(512, d)),
        grid=(n // tm,),
        in_specs=[row(d), row(ma.shape[1]), row(mb.shape[1]), row(p.shape[1]),
                  _resident(w_out.shape), _resident(g_ple.shape), _resident(w_gate.shape),
                  _resident(w_proj.shape), _resident(g_final.shape)],
        out_specs=row(d),
        out_shape=jax.ShapeDtypeStruct((n, d), F32),
        scratch_shapes=[pltpu.VMEM((tm, d), F32), pltpu.VMEM((tm, d), BF16)],
        compiler_params=_params(("parallel",)),
    )(x, ma, mb, p, w_out, g_ple, w_gate, w_proj, g_final)


def _page_copies(pt_ref, pools, bufs, lane_major, sems, q, chunk, slot, pages_per_chunk):
    copies = []
    for k, (pool, buf, lm) in enumerate(zip(pools, bufs, lane_major)):
        rows, width = pool.shape[2], pool.shape[3]
        for p in range(pages_per_chunk):
            pid = pt_ref[q, chunk * pages_per_chunk + p]
            if lm:
                dst = buf.at[slot, :, pl.ds(p * width, width)]
            else:
                dst = buf.at[slot, pl.ds(p * rows, rows)]
            copies.append(pltpu.make_async_copy(pool.at[0, pid], dst, sems.at[slot, k]))
    return copies


def _sample_pass1_kernel(pt_ref, qlat_ref, qpe_ref, qidx_ref, wi_ref, ckvn_ref, kpen_ref, kidxn_ref,
                         ga_ref, wuv_ref, pool_ckv, pool_kpe, pool_kidx,
                         out_ref, score_ref,
                         buf_ckv, buf_kpe, buf_kidx, sems, m_scr, l_scr, acc_scr,
                         *, n_chunks, pages_per_chunk, page, past):
    g = pl.program_id(0)
    n_steps = pl.num_programs(0)
    q = g // n_chunks
    c = g % n_chunks
    slot = g % 2
    pools = (pool_ckv, pool_kpe, pool_kidx)
    bufs = (buf_ckv, buf_kpe, buf_kidx)
    ck = pages_per_chunk * page

    def copies(step, slot_):
        return _page_copies(pt_ref, pools, bufs, (False, True, True), sems, step // n_chunks,
                            step % n_chunks, slot_, pages_per_chunk)

    @pl.when(g == 0)
    def _():
        for cp in copies(g, slot):
            cp.start()

    @pl.when(g + 1 < n_steps)
    def _():
        for cp in copies(g + 1, 1 - slot):
            cp.start()

    qlat = qlat_ref[0].astype(F32)
    qpe = qpe_ref[0].astype(F32)
    qidx = qidx_ref[0].astype(F32)
    wcol = wi_ref[0]

    @pl.when(c == 0)
    def _():
        ckvn = ckvn_ref[0]
        s_new = (jnp.sum(qlat * ckvn, axis=-1, keepdims=True)
                 + jnp.sum(qpe * kpen_ref[0], axis=-1, keepdims=True)) * MLA_SCALE
        m_scr[...] = s_new
        l_scr[...] = jnp.ones_like(s_new)
        acc_scr[...] = jnp.broadcast_to(ckvn, acc_scr.shape)
        si = jnp.maximum(jnp.sum(qidx * kidxn_ref[0], axis=-1, keepdims=True), 0.0) * wcol
        si = jnp.sum(si, axis=0, keepdims=True)
        lane = lax.broadcasted_iota(I32, (1, SCORE_PAD), 1)
        score_ref[0, :, past:] = jnp.where(lane == 0, si, -jnp.inf)

    for cp in copies(g, slot):
        cp.wait()

    kc = buf_ckv[slot]
    s = (_dot_nt(qlat, kc) + _dot(qpe, buf_kpe[slot])) * MLA_SCALE
    m, l, acc = _flash_update((m_scr[...], l_scr[...], acc_scr[...]), s, None, kc)
    m_scr[...] = m
    l_scr[...] = l
    acc_scr[...] = acc

    si = jnp.maximum(_dot(qidx, buf_kidx[slot]), 0.0) * wcol
    score_ref[0, :, pl.ds(pl.multiple_of(c * ck, ck), ck)] = jnp.sum(si, axis=0, keepdims=True)

    @pl.when(c == n_chunks - 1)
    def _():
        lat = (acc / l).astype(BF16)
        for hh in range(A_HEADS):
            o = _dot(lat, wuv_ref[hh])[hh:hh + 1]
            cols = slice(hh * A_V, (hh + 1) * A_V)
            out_ref[0, :, cols] = o * _silu(ga_ref[0, :, cols])


def _sample_pass1(page_table, qlat, qpe, qidx, wi, ckvn, kpen, kidxn, gate_a, w_uv,
                  pool_ckv, pool_kpe, pool_kidx, pages_per_chunk):
    nq, n_pages = page_table.shape
    page = pool_ckv.shape[2]
    past = n_pages * page
    n_chunks = n_pages // pages_per_chunk
    ck = pages_per_chunk * page
    per_q = lambda a: pl.BlockSpec((1,) + a.shape[1:], lambda g, pt: (g // n_chunks, 0, 0))
    any_spec = pl.BlockSpec(memory_space=pl.ANY)
    grid_spec = pltpu.PrefetchScalarGridSpec(
        num_scalar_prefetch=1,
        grid=(nq * n_chunks,),
        in_specs=[per_q(qlat), per_q(qpe), per_q(qidx), per_q(wi), per_q(ckvn), per_q(kpen),
                  per_q(kidxn), per_q(gate_a),
                  pl.BlockSpec(w_uv.shape, lambda g, pt: (0, 0, 0), pipeline_mode=pl.Buffered(1)),
                  any_spec, any_spec, any_spec],
        out_specs=[pl.BlockSpec((1, 1, A_WIDTH), lambda g, pt: (g // n_chunks, 0, 0)),
                   pl.BlockSpec((1, 1, past + SCORE_PAD), lambda g, pt: (g // n_chunks, 0, 0))],
        scratch_shapes=[pltpu.VMEM((2, ck, KV_LORA), F32), pltpu.VMEM((2, A_ROPE, ck), F32),
                        pltpu.VMEM((2, IDX_DIM, ck), F32), pltpu.SemaphoreType.DMA((2, 3)),
                        pltpu.VMEM((A_HEADS, 1), F32), pltpu.VMEM((A_HEADS, 1), F32),
                        pltpu.VMEM((A_HEADS, KV_LORA), F32)],
    )
    return pl.pallas_call(
        functools.partial(_sample_pass1_kernel, n_chunks=n_chunks, pages_per_chunk=pages_per_chunk,
                          page=page, past=past),
        grid_spec=grid_spec,
        out_shape=[jax.ShapeDtypeStruct((nq, 1, A_WIDTH), F32),
                   jax.ShapeDtypeStruct((nq, 1, past + SCORE_PAD), F32)],
        compiler_params=_params(("arbitrary",)),
    )(page_table, qlat, qpe, qidx, wi, ckvn, kpen, kidxn, gate_a, w_uv, pool_ckv, pool_kpe, pool_kidx)


def _topk_thresh_kernel(score_ref, thr_ref, *, k_top):
    key = _sort_key(score_ref[...])

    def count_ge(cand):
        ge = (key >= cand).astype(F32)
        return jnp.sum(jnp.sum(ge, axis=2, keepdims=True), axis=1, keepdims=True)

    thr = _kth_largest_key(count_ge, (key.shape[0], 1, 1), k_top)
    thr_ref[...] = jnp.broadcast_to(thr, thr_ref.shape)


def _topk_thresh(score3, k_top):
    nq = score3.shape[0]
    return pl.pallas_call(
        functools.partial(_topk_thresh_kernel, k_top=k_top),
        in_specs=[pl.BlockSpec(memory_space=pltpu.VMEM)],
        out_specs=pl.BlockSpec(memory_space=pltpu.VMEM),
        out_shape=jax.ShapeDtypeStruct((nq, 1, LANES), I32),
        compiler_params=_params(),
    )(score3)


ROW_TILE = 8
SLOTS_PER_TILE = ROW_TILE // B_KV_HEADS


def _select_rows_kernel(score_ref, thr_ref, pt_ref, tix_ref, sub_ref, nsel_ref, *, n_pages, page, n_rows):
    key = _sort_key(score_ref[0, :n_pages, :])
    prow = lax.broadcasted_iota(I32, (n_pages, page), 0)
    m = ((key >= thr_ref[0][:, :1]) & (prow < n_pages - 1)).astype(F32)
    r_i = lax.broadcasted_iota(I32, (n_pages, n_pages), 0)
    c_i = lax.broadcasted_iota(I32, (n_pages, n_pages), 1)
    cnt = _dot(m.astype(BF16), jnp.ones((page, LANES), BF16))
    o_excl = _dot((c_i < r_i).astype(BF16), cnt.astype(BF16))
    o_incl = o_excl + cnt
    reps = n_rows // LANES
    o_ex = jnp.concatenate([o_excl] * reps, axis=1)
    o_in = jnp.concatenate([o_incl] * reps, axis=1)
    jrow = lax.broadcasted_iota(I32, (n_pages, n_rows), 1).astype(F32)
    hit = ((o_ex <= jrow) & (jrow < o_in)).astype(F32)
    pt_j = jnp.sum(hit * pt_ref[0], axis=0, keepdims=True)
    o_j = jnp.sum(hit * o_ex, axis=0, keepdims=True)
    used = jnp.sum(hit, axis=0, keepdims=True)
    rank = jrow[:1] - o_j
    s_r = lax.broadcasted_iota(I32, (page, page), 0)
    s_c = lax.broadcasted_iota(I32, (page, page), 1)
    pst = _dot((s_c <= s_r).astype(BF16), m.T.astype(BF16))
    psj = _dot(pst.astype(BF16), hit.astype(BF16))
    slot_j = jnp.sum((psj <= rank).astype(F32), axis=0, keepdims=True)
    quad = jnp.floor(slot_j * (1.0 / SLOTS_PER_TILE))
    tile = pt_j * (page // SLOTS_PER_TILE) + quad
    tix_ref[0] = jnp.where(used > 0.0, tile, 0.0).astype(I32)
    sub_ref[0] = ((slot_j - SLOTS_PER_TILE * quad) * B_KV_HEADS).astype(I32)
    nsel_ref[0] = o_incl[n_pages - 1:n_pages, :].astype(I32)


def _select_rows(score3, thr, pt_col, n_rows):
    nq, rows, page = score3.shape
    n_pages = pt_col.shape[1]
    assert n_rows % LANES == 0 and page == LANES and n_pages % 8 == 0
    per_q = lambda a: pl.BlockSpec((1,) + a.shape[1:], lambda q: (q, 0, 0))
    out = lambda w: pl.BlockSpec((1, 1, w), lambda q: (q, 0, 0))
    return pl.pallas_call(
        functools.partial(_select_rows_kernel, n_pages=n_pages, page=page, n_rows=n_rows),
        grid=(nq,),
        in_specs=[per_q(score3), per_q(thr), per_q(pt_col)],
        out_specs=[out(n_rows), out(n_rows), out(LANES)],
        out_shape=[jax.ShapeDtypeStruct((nq, 1, n_rows), I32), jax.ShapeDtypeStruct((nq, 1, n_rows), I32),
                   jax.ShapeDtypeStruct((nq, 1, LANES), I32)],
        compiler_params=_params(("parallel",)),
    )(score3, thr, pt_col)


def _sample_pass2_kernel(pt_ref, tix_ref, cfar_ref, qb_ref, kbn_ref, vbn_ref, gb_ref, sub_ref, nsel_ref,
                         slast_ref, snew_ref, thr_ref, tail_ref, bnew_ref, pool_k, pool_v, out_ref,
                         buf_k, buf_v, sems, *, n_pages, page, n_rows):
    q = pl.program_id(0)
    nq = pl.num_programs(0)
    slot = q % 2
    gw = ROW_TILE * n_rows
    lw = B_KV_HEADS * page
    pools_bufs = ((pool_k, buf_k), (pool_v, buf_v))

    def start_all(qq, slot_):
        for k, (pool, buf) in enumerate(pools_bufs):
            for j in range(n_rows):
                off = pl.multiple_of(tix_ref[qq, j] * ROW_TILE, ROW_TILE)
                pltpu.make_async_copy(pool.at[pl.ds(off, ROW_TILE)],
                                      buf.at[slot_, pl.ds(j * ROW_TILE, ROW_TILE)], sems.at[slot_, k]).start()
            last = pl.multiple_of(pt_ref[qq, n_pages - 1] * lw, lw)
            pltpu.make_async_copy(pool.at[pl.ds(last, lw)], buf.at[slot_, pl.ds(gw, lw)],
                                  sems.at[slot_, k]).start()

    @pl.when(q == 0)
    def _():
        start_all(q, slot)

    @pl.when(q + 1 < nq)
    def _():
        start_all(q + 1, 1 - slot)

    qb = qb_ref[0].astype(F32)
    thr = thr_ref[0][:, :1]
    kv_of_row = (lax.broadcasted_iota(I32, (B_HEADS, 1), 0) >= B_GROUP).astype(I32)
    head0 = kv_of_row == 0
    lo, hi = slice(0, B_HD), slice(B_HD, 2 * B_HD)

    keep = _sort_key(snew_ref[0, :, :LANES])[:, :1] >= thr
    kbn = kbn_ref[0]
    vbn = vbn_ref[0]
    s_new = jnp.where(head0, jnp.sum(qb * kbn[:, lo], axis=-1, keepdims=True),
                      jnp.sum(qb * kbn[:, hi], axis=-1, keepdims=True))
    s_new = jnp.where(keep, s_new * B_SCALE + bnew_ref[...], NEG)
    v_new = jnp.where(head0, jnp.broadcast_to(vbn[:, lo], (B_HEADS, B_HD)),
                      jnp.broadcast_to(vbn[:, hi], (B_HEADS, B_HD)))

    lane_g = lax.broadcasted_iota(I32, (B_HEADS, gw), 1)
    valid_g = ((lane_g & (ROW_TILE - 1)) == sub_ref[0] + kv_of_row) & (lane_g < ROW_TILE * nsel_ref[0][:, :1])
    lane_l = lax.broadcasted_iota(I32, (B_HEADS, lw), 1)
    sel_l = _sort_key(slast_ref[0]) >= thr
    valid_l = jnp.broadcast_to(sel_l, (B_HEADS, lw)) & ((lane_l & 1) == kv_of_row)
    valid = jnp.concatenate([valid_g, valid_l], axis=1)
    bias = jnp.concatenate([jnp.broadcast_to(cfar_ref[...], (B_HEADS, gw)), tail_ref[...]], axis=1)

    for k, (pool, buf) in enumerate(pools_bufs):
        pltpu.make_async_copy(pool.at[pl.ds(0, gw + lw)], buf.at[slot], sems.at[slot, k]).wait()

    s = _dot_nt(qb, buf_k[slot]) * B_SCALE + bias
    s = jnp.where(valid, s, NEG)
    m = jnp.maximum(jnp.max(s, axis=-1, keepdims=True), s_new)
    p = jnp.where(valid, jnp.exp(s - m), 0.0)
    p_new = jnp.where(keep, jnp.exp(s_new - m), 0.0)
    l = jnp.sum(p, axis=-1, keepdims=True) + p_new
    o = (_dot(p, buf_v[slot]) + p_new * v_new) / l
    for hh in range(B_HEADS):
        cols = slice(hh * B_HD, (hh + 1) * B_HD)
        out_ref[0, :, cols] = o[hh:hh + 1] * _silu(gb_ref[0, :, cols])


def _sample_pass2(page_table, tix, cfar, qb, kbn, vbn, gate_b, sub_rep, nsel, slast2, score, thr, tail2,
                  bias_new, pool_k2, pool_v2, page):
    nq, n_pages = page_table.shape
    n_rows = tix.shape[1]
    past = n_pages * page
    rows = ROW_TILE * n_rows + B_KV_HEADS * page
    assert past % SCORE_PAD == 0
    per_q = lambda a: pl.BlockSpec((1,) + a.shape[1:], lambda q, pt, tx: (q, 0, 0))
    const2 = lambda a: pl.BlockSpec(a.shape, lambda q, pt, tx: (0, 0))
    any_spec = pl.BlockSpec(memory_space=pl.ANY)
    grid_spec = pltpu.PrefetchScalarGridSpec(
        num_scalar_prefetch=2,
        grid=(nq,),
        in_specs=[const2(cfar), per_q(qb), per_q(kbn), per_q(vbn), per_q(gate_b), per_q(sub_rep),
                  per_q(nsel), per_q(slast2),
                  pl.BlockSpec((1, 1, SCORE_PAD), lambda q, pt, tx: (q, 0, past // SCORE_PAD)),
                  per_q(thr), const2(tail2), const2(bias_new), any_spec, any_spec],
        out_specs=pl.BlockSpec((1, 1, B_WIDTH), lambda q, pt, tx: (q, 0, 0)),
        scratch_shapes=[pltpu.VMEM((2, rows, B_HD), F32), pltpu.VMEM((2, rows, B_HD), F32),
                        pltpu.SemaphoreType.DMA((2, 2))],
    )
    return pl.pallas_call(
        functools.partial(_sample_pass2_kernel, n_pages=n_pages, page=page, n_rows=n_rows),
        grid_spec=grid_spec,
        out_shape=jax.ShapeDtypeStruct((nq, 1, B_WIDTH), F32),
        compiler_params=_params(("arbitrary",)),
    )(page_table, tix, cfar, qb, kbn, vbn, gate_b, sub_rep, nsel, slast2, score, thr, tail2, bias_new,
      pool_k2, pool_v2)


def _prep_w_in(w_in):
    offs = np.concatenate([[0], np.cumsum(IN_SIZES)])
    part = lambda k: w_in[:, offs[k]:offs[k + 1]]
    c_q, c_kv, k_pe, gate_a, q_b, k_b, v_b, gate_b, q_idx, k_idx, w_idx = [part(k) for k in range(11)]
    pad = jnp.zeros((w_in.shape[0], LANES - IDX_HEADS), w_in.dtype)
    cols = [c_q, c_kv, gate_a, q_b, k_b, v_b, gate_b, q_idx, k_pe, k_idx, w_idx, pad]
    return jnp.concatenate(cols, axis=1).astype(BF16)


def _prep_w_uq(w_uq):
    w = w_uq.reshape(w_uq.shape[0], A_HEADS, A_NOPE + A_ROPE)
    nope = w[:, :, :A_NOPE].reshape(w_uq.shape[0], A_HEADS * A_NOPE)
    rope = w[:, :, A_NOPE:].reshape(w_uq.shape[0], A_HEADS * A_ROPE)
    return jnp.concatenate([nope, rope], axis=1).astype(BF16)


def kernel(x_prompt, x_sample, p_prompt, p_sample, cache_ckv, cache_kpe, cache_k, cache_v, cache_kidx,
           page_table, rel_table, g_attn, w_in, g_q, w_uq, g_kv, w_uk, w_uv, w_out, g_ple, w_ple_gate,
           w_ple_proj, g_final):
    nb, seq, d = x_prompt.shape
    nq, nt, _ = x_sample.shape
    depth = w_in.shape[0]
    assert depth == 1 and nt == 1
    n_pages = page_table.shape[1]
    page = cache_ckv.shape[2]
    past = n_pages * page
    tq = min(256, seq)
    tm_s = min(128, nq)
    pages_per_chunk = min(32, n_pages)
    assert seq % tq == 0 and nq % tm_s == 0 and n_pages % pages_per_chunk == 0
    assert tq + 1 > REL_MAX_DIST

    row2 = lambda v: v.reshape(1, -1)
    w_in_p = _prep_w_in(w_in[0])
    w_uq_p = _prep_w_uq(w_uq[0])
    w_uk_b = w_uk[0].astype(BF16)
    w_uv_b = w_uv[0].astype(BF16)
    w_out_b = w_out[0].astype(BF16)
    w_gate_b = w_ple_gate[0].astype(BF16)
    w_proj_b = w_ple_proj[0].astype(BF16)
    cfar = rel_table[REL_BUCKETS - 1]

    tabs_p = _rope_tables(jnp.arange(seq, dtype=I32))
    xp = x_prompt.reshape(nb * seq, d)
    (qcat, qi2, _, _, wi, qb, ga, gb, ckv, kpe, kb, vb, kidx, kcat, ii, kbb, vbb) = _proj_in(
        xp, tabs_p, seq // tq, row2(g_attn[0]), w_in_p, row2(g_q[0]), w_uq_p, row2(g_kv[0]), w_uk_b, tq)
    b3 = lambda a: a.reshape((nb, seq) + a.shape[1:])
    b4 = lambda a: a.reshape((a.shape[0], nb, seq) + a.shape[2:])
    ma = _mla_prompt(b4(qcat), b3(kcat), b3(ga), w_uv_b, tq)
    rel = np.arange(tq)[:, None] + tq - np.arange(2 * tq)[None, :]
    bias = _bias_tiles(rel_table, rel)
    k_top = min(IDX_TOPK, seq // 4)
    mb = _dsa_prompt(cfar, b4(qi2), b3(wi), b3(ii), b3(qb), b3(kbb), b3(vbb), b3(gb), bias, tq, k_top)
    y_prompt = _out_proj(xp, ma.reshape(nb * seq, -1), mb.reshape(nb * seq, -1),
                         p_prompt[0].reshape(nb * seq, -1), w_out_b, row2(g_ple[0]), w_gate_b, w_proj_b,
                         row2(g_final), tq).reshape(nb, seq, d)

    tabs_s = _rope_tables(jnp.full((tm_s,), past, I32))
    xs = x_sample.reshape(nq, d)
    (qcat_s, _, qpe_s, qidx_s, wi_s, qb_s, ga_s, gb_s, ckv_s, kpe_s, kb_s, vb_s, kidx_s, _, _, _, _) = _proj_in(
        xs, tabs_s, 1, row2(g_attn[0]), w_in_p, row2(g_q[0]), w_uq_p, row2(g_kv[0]), w_uk_b, tm_s)
    qlat_s = jnp.swapaxes(qcat_s[:, :, :KV_LORA], 0, 1)
    ma_s, score = _sample_pass1(
        page_table, qlat_s, qpe_s.reshape(nq, A_HEADS, A_ROPE), qidx_s.reshape(nq, IDX_HEADS, IDX_DIM),
        wi_s[:, :IDX_HEADS].reshape(nq, IDX_HEADS, 1), ckv_s.reshape(nq, 1, -1), kpe_s.reshape(nq, 1, -1),
        kidx_s.reshape(nq, 1, -1), ga_s.reshape(nq, 1, -1), w_uv_b, cache_ckv,
        jnp.swapaxes(cache_kpe, 2, 3), jnp.swapaxes(cache_kidx, 2, 3), pages_per_chunk)
    k_top_s = min(IDX_TOPK, (past + nt) // 4)
    score3 = score.reshape(nq, n_pages + SCORE_PAD // LANES, LANES)
    thr = _topk_thresh(score3, k_top_s)
    tix, sub, nsel = _select_rows(score3, thr, page_table.astype(F32).reshape(nq, n_pages, 1), k_top_s)
    rel_tail = np.broadcast_to(np.concatenate(
        [np.repeat(page - np.arange(page), 2), np.zeros(LANES, np.int64)])[None, :], (8, 2 * page + LANES))
    tail = _bias_tiles(rel_table, rel_tail)[:, 0, :]
    rows2 = lambda pool: pool.reshape(pool.shape[1] * B_KV_HEADS * page, B_HD)
    mb_s = _sample_pass2(page_table, tix.reshape(nq, -1), cfar.reshape(B_HEADS, 1),
                         qb_s.reshape(nq, B_HEADS, B_HD), kb_s.reshape(nq, 1, -1), vb_s.reshape(nq, 1, -1),
                         gb_s.reshape(nq, 1, -1), jnp.repeat(sub, ROW_TILE, axis=2), nsel,
                         jnp.repeat(score[:, :, past - page:past], 2, axis=2), score, thr,
                         tail[:, :2 * page], tail[:, 2 * page:2 * page + 1],
                         rows2(cache_k), rows2(cache_v), page)
    y_sample = _out_proj(xs, ma_s.reshape(nq, -1).astype(BF16), mb_s.reshape(nq, -1).astype(BF16),
                         p_sample[0].reshape(nq, -1), w_out_b, row2(g_ple[0]), w_gate_b, w_proj_b,
                         row2(g_final), tm_s).reshape(nq, nt, d)

    return (y_prompt, y_sample,
            ckv.reshape(1, nb, seq, -1), kpe.reshape(1, nb, seq, -1),
            kb.reshape(1, nb, seq, B_KV_HEADS, B_HD), vb.reshape(1, nb, seq, B_KV_HEADS, B_HD),
            kidx.reshape(1, nb, seq, -1),
            ckv_s.reshape(1, nq, nt, -1), kpe_s.reshape(1, nq, nt, -1),
            kb_s.reshape(1, nq, nt, B_KV_HEADS, B_HD), vb_s.reshape(1, nq, nt, B_KV_HEADS, B_HD),
            kidx_s.reshape(1, nq, nt, -1))
```

```python
import functools
import math

import numpy as np
import jax
import jax.numpy as jnp
from jax import lax
from jax.experimental import pallas as pl
from jax.experimental.pallas import tpu as pltpu

F32 = jnp.float32
BF16 = jnp.bfloat16
I32 = jnp.int32

A_HEADS = 8
A_NOPE = 128
A_ROPE = 64
A_V = 128
A_WIDTH = A_HEADS * A_V
Q_LORA = 512
KV_LORA = 256
B_HEADS = 8
B_KV_HEADS = 2
B_HD = 128
B_GROUP = B_HEADS // B_KV_HEADS
B_WIDTH = B_HEADS * B_HD
B_KVW = B_KV_HEADS * B_HD
IDX_HEADS = 16
IDX_DIM = 64
IDX_ROPE = 32
IDX_TOPK = 256
REL_BUCKETS = 32
REL_MAX_DIST = 128
ROPE_THETA = 10000.0
EPS = 1e-6
MLA_SCALE = (A_NOPE + A_ROPE) ** -0.5
B_SCALE = B_HD ** -0.5
IDX_W_SCALE = (IDX_HEADS ** -0.5) * (IDX_DIM ** -0.5)
IN_SIZES = (Q_LORA, KV_LORA, A_ROPE, A_WIDTH, B_WIDTH, B_KVW, B_KVW, B_WIDTH,
            IDX_HEADS * IDX_DIM, IDX_DIM, IDX_HEADS)

LANES = 128
NEG = -1e30
VMEM_LIMIT = 56 * 1024 * 1024
INT_MIN = -2 ** 31

C_Q = 0
C_KV = C_Q + Q_LORA
C_GA = C_KV + KV_LORA
C_QB = C_GA + A_WIDTH
C_KB = C_QB + B_WIDTH
C_VB = C_KB + B_KVW
C_GB = C_VB + B_KVW
C_QI = C_GB + B_WIDTH
C_SLAB = C_QI + IDX_HEADS * IDX_DIM
C_WI = C_SLAB + LANES
NP_IN = C_WI + LANES
QCAT = KV_LORA + LANES
N_TAB = 11
SCORE_PAD = 8 * LANES


def _dot(a, b):
    return jnp.dot(a, b, preferred_element_type=F32)


def _dot_nt(a, b):
    return lax.dot_general(a, b, (((1,), (1,)), ((), ())), preferred_element_type=F32)


def _silu(x):
    return x / (1.0 + jnp.exp(-x))


def _params(sem=None):
    return pltpu.CompilerParams(dimension_semantics=sem, vmem_limit_bytes=VMEM_LIMIT)


def _resident(shape):
    nd = len(shape)
    return pl.BlockSpec(shape, lambda *_: (0,) * nd, pipeline_mode=pl.Buffered(1))


def _rope_tables(pos):
    def cs(half):
        inv = ROPE_THETA ** (-jnp.arange(half, dtype=F32) / half)
        ang = pos.astype(F32)[:, None] * inv[None, :]
        return jnp.cos(ang), jnp.sin(ang)
    n = pos.shape[0]
    c32, s32 = cs(A_ROPE // 2)
    c16, s16 = cs(IDX_ROPE // 2)
    z16 = jnp.zeros((n, 16), F32)
    z32 = jnp.zeros((n, 32), F32)
    z64 = jnp.zeros((n, 64), F32)
    o32 = jnp.ones((n, 32), F32)
    cat = lambda *a: jnp.concatenate(a, axis=1)
    a64, bm64, bp64 = cat(c32, c32), cat(-s32, z32), cat(z32, s32)
    a16, cm16, cp16 = cat(c16, c16, o32), cat(-s16, z16, z32), cat(z16, s16, z32)
    tabs = [cat(a64, a64), cat(bm64, bm64), cat(bp64, bp64),
            cat(a16, a16), cat(cm16, cm16), cat(cp16, cp16),
            cat(a64, a16), cat(bm64, z64), cat(bp64, z64), cat(z64, cm16), cat(z64, cp16)]
    return jnp.concatenate(tabs, axis=1)


def _proj_in_kernel(x_ref, gattn_ref, win_ref, gq_ref, wuq_ref, gkv_ref, wuk_ref, tab_ref,
                    qcat_ref, qi2_ref, qpe_ref, qidx_ref, wi_ref, qb_ref, ga_ref, gb_ref,
                    ckv_ref, kpe_ref, kb_ref, vb_ref, kidx_ref, kcat_ref, ii_ref, kbb_ref, vbb_ref,
                    h_scr):
    x = x_ref[...]
    ms = jnp.mean(x * x, axis=-1, keepdims=True)
    h_scr[...] = (x * lax.rsqrt(ms + EPS) * gattn_ref[...]).astype(BF16)

    def zcols(c0, width):
        return _dot(h_scr[...], win_ref[:, c0:c0 + width])

    def tab(k):
        return tab_ref[:, k * LANES:(k + 1) * LANES]

    tm = x.shape[0]
    lane = lax.broadcasted_iota(I32, (tm, LANES), 1)
    lo = lane < 64

    def rope(xg, a, bm, bp, shift):
        return (xg * tab(a) + pltpu.roll(xg, LANES - shift, 1) * tab(bm)
                + pltpu.roll(xg, shift, 1) * tab(bp))

    cq = zcols(C_Q, Q_LORA)
    cq = cq * lax.rsqrt(jnp.mean(cq * cq, axis=-1, keepdims=True) + EPS) * gq_ref[...]
    q = _dot(cq.astype(BF16), wuq_ref[...])
    n_nope = A_HEADS * A_NOPE
    for g in range(A_HEADS // 2):
        xg = q[:, n_nope + g * LANES:n_nope + (g + 1) * LANES]
        rg = rope(xg, 0, 1, 2, A_ROPE // 2)
        qpe_ref[:, g * LANES:(g + 1) * LANES] = rg.astype(BF16)
        qcat_ref[2 * g, :, KV_LORA:] = jnp.where(lo, rg, 0.0).astype(BF16)
        qcat_ref[2 * g + 1, :, KV_LORA:] = jnp.where(lo, 0.0, rg).astype(BF16)
    for hh in range(A_HEADS):
        qn = q[:, hh * A_NOPE:(hh + 1) * A_NOPE].astype(BF16)
        qcat_ref[hh, :, :KV_LORA] = _dot(qn, wuk_ref[hh]).astype(BF16)

    ckv = zcols(C_KV, KV_LORA)
    ckv = ckv * lax.rsqrt(jnp.mean(ckv * ckv, axis=-1, keepdims=True) + EPS) * gkv_ref[...]
    ckv_ref[...] = ckv
    kcat_ref[:, :KV_LORA] = ckv.astype(BF16)

    slab = zcols(C_SLAB, LANES)
    slab = (slab * tab(6) + pltpu.roll(slab, LANES - 32, 1) * tab(7) + pltpu.roll(slab, 32, 1) * tab(8)
            + pltpu.roll(slab, LANES - 16, 1) * tab(9) + pltpu.roll(slab, 16, 1) * tab(10))
    swapped = pltpu.roll(slab, 64, 1)
    kpe_ref[...] = slab[:, :A_ROPE]
    kidx_ref[...] = swapped[:, :IDX_DIM]
    kcat_ref[:, KV_LORA:] = jnp.where(lo, slab, swapped).astype(BF16)
    ii_ref[...] = jnp.where(lo, swapped, slab).astype(BF16)

    kb = zcols(C_KB, B_KVW)
    kb_ref[...] = kb
    kbb_ref[...] = kb.astype(BF16)
    vb = zcols(C_VB, B_KVW)
    vb_ref[...] = vb
    vbb_ref[...] = vb.astype(BF16)
    for c in range(B_WIDTH // 512):
        qb_ref[:, c * 512:(c + 1) * 512] = zcols(C_QB + c * 512, 512).astype(BF16)
        ga_ref[:, c * 512:(c + 1) * 512] = zcols(C_GA + c * 512, 512)
        gb_ref[:, c * 512:(c + 1) * 512] = zcols(C_GB + c * 512, 512)

    for c in range(IDX_HEADS * IDX_DIM // 512):
        qi = zcols(C_QI + c * 512, 512)
        for g in range(4):
            rg = rope(qi[:, g * LANES:(g + 1) * LANES], 3, 4, 5, IDX_ROPE // 2)
            col = c * 512 + g * LANES
            qidx_ref[:, col:col + LANES] = rg.astype(BF16)
            hh = col // IDX_DIM
            qi2_ref[hh] = jnp.where(lo, rg, 0.0).astype(BF16)
            qi2_ref[hh + 1] = jnp.where(lo, 0.0, rg).astype(BF16)
    wi_ref[...] = zcols(C_WI, LANES) * IDX_W_SCALE


def _proj_in(x, tabs, tab_blocks, g_attn, w_in_p, g_q, w_uq_p, g_kv, w_uk, tm):
    n, d = x.shape
    grid = (n // tm,)
    row = lambda w: pl.BlockSpec((tm, w), lambda i: (i, 0))
    head = lambda nh, w: pl.BlockSpec((nh, tm, w), lambda i: (0, i, 0))
    out_shape = [
        jax.ShapeDtypeStruct((A_HEADS, n, QCAT), BF16),
        jax.ShapeDtypeStruct((IDX_HEADS, n, LANES), BF16),
        jax.ShapeDtypeStruct((n, A_HEADS * A_ROPE), BF16),
        jax.ShapeDtypeStruct((n, IDX_HEADS * IDX_DIM), BF16),
        jax.ShapeDtypeStruct((n, LANES), F32),
        jax.ShapeDtypeStruct((n, B_WIDTH), BF16),
        jax.ShapeDtypeStruct((n, A_WIDTH), F32),
        jax.ShapeDtypeStruct((n, B_WIDTH), F32),
        jax.ShapeDtypeStruct((n, KV_LORA), F32),
        jax.ShapeDtypeStruct((n, A_ROPE), F32),
        jax.ShapeDtypeStruct((n, B_KVW), F32),
        jax.ShapeDtypeStruct((n, B_KVW), F32),
        jax.ShapeDtypeStruct((n, IDX_DIM), F32),
        jax.ShapeDtypeStruct((n, QCAT), BF16),
        jax.ShapeDtypeStruct((n, LANES), BF16),
        jax.ShapeDtypeStruct((n, B_KVW), BF16),
        jax.ShapeDtypeStruct((n, B_KVW), BF16),
    ]
    out_specs = [head(A_HEADS, QCAT), head(IDX_HEADS, LANES), row(A_HEADS * A_ROPE),
                 row(IDX_HEADS * IDX_DIM), row(LANES), row(B_WIDTH), row(A_WIDTH), row(B_WIDTH),
                 row(KV_LORA), row(A_ROPE), row(B_KVW), row(B_KVW), row(IDX_DIM),
                 row(QCAT), row(LANES), row(B_KVW), row(B_KVW)]
    in_specs = [row(d), _resident(g_attn.shape), _resident(w_in_p.shape), _resident(g_q.shape),
                _resident(w_uq_p.shape), _resident(g_kv.shape), _resident(w_uk.shape),
                pl.BlockSpec((tm, N_TAB * LANES), lambda i: (i % tab_blocks, 0))]
    return pl.pallas_call(
        _proj_in_kernel, grid=grid, in_specs=in_specs, out_specs=out_specs, out_shape=out_shape,
        scratch_shapes=[pltpu.VMEM((tm, d), BF16)],
        compiler_params=_params(("parallel",)),
    )(x, g_attn, w_in_p, g_q, w_uq_p, g_kv, w_uk, tabs)


def _flash_update(carry, s, valid, v):
    m, l, acc = carry
    if valid is not None:
        s = jnp.where(valid, s, NEG)
    m_new = jnp.maximum(m, jnp.max(s, axis=-1, keepdims=True))
    p = jnp.exp(s - m_new)
    if valid is not None:
        p = jnp.where(valid, p, 0.0)
    alpha = jnp.exp(m - m_new)
    l = alpha * l + jnp.sum(p, axis=-1, keepdims=True)
    acc = alpha * acc + _dot(p.astype(v.dtype), v)
    return m_new, l, acc


def _fold_lanes(x, op):
    out = x[:, :LANES]
    for u in range(1, x.shape[1] // LANES):
        out = op(out, x[:, u * LANES:(u + 1) * LANES])
    return out


def _row_max_all_lanes(mrun_scr, hh):
    m = jnp.max(mrun_scr[hh], axis=-1, keepdims=True)
    mrun_scr[hh] = jnp.broadcast_to(m, mrun_scr.shape[1:])


def _softmax_value_step(s_scr, mrun_scr, lrun_scr, acc_scr, hh, j, v):
    m = mrun_scr[hh]
    p = jnp.exp(s_scr[hh, j] - jnp.concatenate([m] * (s_scr.shape[3] // LANES), axis=1))
    lrun_scr[hh] = lrun_scr[hh] + _fold_lanes(p, jnp.add)
    acc_scr[hh] = acc_scr[hh] + _dot(p.astype(v.dtype), v)


def _mla_prompt_kernel(qcat_ref, kcat_ref, ga_ref, wuv_ref, out_ref, s_scr, mrun_scr, lrun_scr, acc_scr,
                       *, tq):
    i = pl.program_id(1)
    row = lax.broadcasted_iota(I32, (tq, tq), 0)
    col = lax.broadcasted_iota(I32, (tq, tq), 1)
    causal = col <= row
    mrun_scr[...] = jnp.full(mrun_scr.shape, NEG, F32)

    def score_chunk(j, valid):
        kc = kcat_ref[0, pl.ds(pl.multiple_of(j * tq, tq), tq), :]
        for hh in range(A_HEADS):
            s = _dot_nt(qcat_ref[hh, 0], kc) * MLA_SCALE
            if valid is not None:
                s = jnp.where(valid, s, NEG)
            s_scr[hh, j] = s
            mrun_scr[hh] = jnp.maximum(mrun_scr[hh], _fold_lanes(s, jnp.maximum))

    def far(j, carry):
        score_chunk(j, None)
        return carry

    lax.fori_loop(0, i, far, 0)
    score_chunk(i, causal)
    for hh in range(A_HEADS):
        _row_max_all_lanes(mrun_scr, hh)
    lrun_scr[...] = jnp.zeros(lrun_scr.shape, F32)
    acc_scr[...] = jnp.zeros(acc_scr.shape, F32)

    def value_chunk(j, carry):
        v = kcat_ref[0, pl.ds(pl.multiple_of(j * tq, tq), tq), :KV_LORA]
        for hh in range(A_HEADS):
            _softmax_value_step(s_scr, mrun_scr, lrun_scr, acc_scr, hh, j, v)
        return carry

    lax.fori_loop(0, i + 1, value_chunk, 0)
    for hh in range(A_HEADS):
        l = jnp.sum(lrun_scr[hh], axis=-1, keepdims=True)
        lat = (acc_scr[hh] / l).astype(BF16)
        o = _dot(lat, wuv_ref[hh])
        cols = slice(hh * A_V, (hh + 1) * A_V)
        out_ref[0, :, cols] = (o * _silu(ga_ref[0, :, cols])).astype(BF16)


def _mla_prompt(qcat, kcat, gate_a, w_uv, tq):
    nb, seq = kcat.shape[0], kcat.shape[1]
    return pl.pallas_call(
        functools.partial(_mla_prompt_kernel, tq=tq),
        grid=(nb, seq // tq),
        in_specs=[pl.BlockSpec((A_HEADS, 1, tq, QCAT), lambda b, i: (0, b, i, 0)),
                  pl.BlockSpec((1, seq, QCAT), lambda b, i: (b, 0, 0)),
                  pl.BlockSpec((1, tq, A_WIDTH), lambda b, i: (b, i, 0)),
                  _resident(w_uv.shape)],
        out_specs=pl.BlockSpec((1, tq, A_WIDTH), lambda b, i: (b, i, 0)),
        out_shape=jax.ShapeDtypeStruct((nb, seq, A_WIDTH), BF16),
        scratch_shapes=[pltpu.VMEM((A_HEADS, seq // tq, tq, tq), F32),
                        pltpu.VMEM((A_HEADS, tq, LANES), F32), pltpu.VMEM((A_HEADS, tq, LANES), F32),
                        pltpu.VMEM((A_HEADS, tq, KV_LORA), F32)],
        compiler_params=_params(("parallel", "parallel")),
    )(qcat, kcat, gate_a, w_uv)


def _sort_key(score):
    b = pltpu.bitcast(score + 0.0, I32)
    return jnp.where(b < 0, b ^ 0x7FFFFFFF, b)


def _kth_largest_key(count_ge, shape, k):
    def body(it, lo):
        bit = 31 - it
        cand = lo + lax.shift_left(jnp.int32(1), bit)
        return jnp.where(count_ge(cand) >= k, cand, lo)
    return lax.fori_loop(0, 32, body, jnp.full(shape, INT_MIN, I32))


def _dsa_prompt_kernel(cfar_ref, qi2_ref, wi_ref, ii_ref, qb_ref, kbb_ref, vbb_ref, gb_ref, bias_ref,
                       out_ref, key_scr, thr_scr, s_scr, mrun_scr, lrun_scr, acc_scr, *, tq, seq, k_top, cw):
    i = pl.program_id(1)
    t0 = i * tq
    w = wi_ref[0]
    qrow = t0 + lax.broadcasted_iota(I32, (tq, cw), 0)
    kcol = lax.broadcasted_iota(I32, (tq, cw), 1)

    for c in range(seq // cw):
        @pl.when(c * cw < t0 + tq)
        def _():
            kc = ii_ref[0, c * cw:(c + 1) * cw, :]
            acc = jnp.zeros((tq, cw), F32)
            for hh in range(IDX_HEADS):
                s = _dot_nt(qi2_ref[hh, 0], kc)
                acc = acc + w[:, hh:hh + 1] * jnp.maximum(s, 0.0)
            acc = jnp.where(kcol + c * cw <= qrow, acc, -jnp.inf)
            key = _sort_key(acc)
            for u in range(cw // tq):
                key_scr[c * (cw // tq) + u] = key[:, u * tq:(u + 1) * tq]

    for n in range(1, seq // tq + 1):
        @pl.when(i == n - 1)
        def _():
            def count_ge(cand):
                part = jnp.zeros((tq, LANES), F32)
                for j in range(n):
                    ge = (key_scr[j] >= cand).astype(F32)
                    for u in range(tq // LANES):
                        part = part + ge[:, u * LANES:(u + 1) * LANES]
                return jnp.sum(part, axis=1, keepdims=True)

            thr_scr[...] = _kth_largest_key(count_ge, (tq, 1), k_top)

    thr = thr_scr[...]

    row = lax.broadcasted_iota(I32, (tq, tq), 0)
    col = lax.broadcasted_iota(I32, (tq, tq), 1)
    causal = col <= row
    mrun_scr[...] = jnp.full(mrun_scr.shape, NEG, F32)

    def score_chunk(j, bias_of, extra):
        off = pl.multiple_of(j * tq, tq)
        sel = key_scr[j] >= thr
        if extra is not None:
            sel = sel & extra
        for hh in range(B_HEADS):
            kv_cols = slice((hh // B_GROUP) * B_HD, (hh // B_GROUP + 1) * B_HD)
            qh = qb_ref[0, :, hh * B_HD:(hh + 1) * B_HD]
            s = _dot_nt(qh, kbb_ref[0, pl.ds(off, tq), kv_cols]) * B_SCALE + bias_of(hh)
            s = jnp.where(sel, s, NEG)
            s_scr[hh, j] = s
            mrun_scr[hh] = jnp.maximum(mrun_scr[hh], _fold_lanes(s, jnp.maximum))

    def far(j, carry):
        score_chunk(j, lambda hh: cfar_ref[hh], None)
        return carry

    lax.fori_loop(0, jnp.maximum(i - 1, 0), far, 0)

    @pl.when(i > 0)
    def _():
        score_chunk(i - 1, lambda hh: bias_ref[hh, :, :tq], None)

    score_chunk(i, lambda hh: bias_ref[hh, :, tq:], causal)
    for hh in range(B_HEADS):
        _row_max_all_lanes(mrun_scr, hh)
    lrun_scr[...] = jnp.zeros(lrun_scr.shape, F32)
    acc_scr[...] = jnp.zeros(acc_scr.shape, F32)

    def value_chunk(j, carry):
        off = pl.multiple_of(j * tq, tq)
        for hh in range(B_HEADS):
            kv_cols = slice((hh // B_GROUP) * B_HD, (hh // B_GROUP + 1) * B_HD)
            _softmax_value_step(s_scr, mrun_scr, lrun_scr, acc_scr, hh, j, vbb_ref[0, pl.ds(off, tq), kv_cols])
        return carry

    lax.fori_loop(0, i + 1, value_chunk, 0)
    for hh in range(B_HEADS):
        cols = slice(hh * B_HD, (hh + 1) * B_HD)
        l = jnp.sum(lrun_scr[hh], axis=-1, keepdims=True)
        out_ref[0, :, cols] = ((acc_scr[hh] / l) * _silu(gb_ref[0, :, cols])).astype(BF16)


def _dsa_prompt(cfar, qi2, wi, ii, qb, kbb, vbb, gate_b, bias, tq, k_top):
    nb, seq = ii.shape[0], ii.shape[1]
    cw = min(512, seq)
    assert cw % tq == 0 and seq % cw == 0 and k_top <= tq
    full = lambda w: pl.BlockSpec((1, seq, w), lambda b, i: (b, 0, 0))
    tile = lambda w: pl.BlockSpec((1, tq, w), lambda b, i: (b, i, 0))
    return pl.pallas_call(
        functools.partial(_dsa_prompt_kernel, tq=tq, seq=seq, k_top=k_top, cw=cw),
        grid=(nb, seq // tq),
        in_specs=[pl.BlockSpec(memory_space=pltpu.SMEM),
                  pl.BlockSpec((IDX_HEADS, 1, tq, LANES), lambda b, i: (0, b, i, 0)),
                  tile(LANES), full(LANES), tile(B_WIDTH), full(B_KVW), full(B_KVW), tile(B_WIDTH),
                  _resident(bias.shape)],
        out_specs=tile(B_WIDTH),
        out_shape=jax.ShapeDtypeStruct((nb, seq, B_WIDTH), BF16),
        scratch_shapes=[pltpu.VMEM((seq // tq, tq, tq), I32), pltpu.VMEM((tq, 1), I32),
                        pltpu.VMEM((B_HEADS, seq // tq, tq, tq), F32),
                        pltpu.VMEM((B_HEADS, tq, LANES), F32), pltpu.VMEM((B_HEADS, tq, LANES), F32),
                        pltpu.VMEM((B_HEADS, tq, B_HD), F32)],
        compiler_params=_params(("parallel", "parallel")),
    )(cfar, qi2, wi, ii, qb, kbb, vbb, gate_b, bias)


def _bias_kernel(table_ref, bucket_ref, out_ref):
    bucket = bucket_ref[...]
    for hh in range(B_HEADS):
        acc = jnp.zeros(bucket.shape, F32)
        for b in range(REL_BUCKETS):
            acc = jnp.where(bucket == b, table_ref[b, hh], acc)
        out_ref[hh] = acc


def _t5_bucket_np(rel):
    n = np.maximum(rel, 0)
    exact = REL_BUCKETS // 2
    nf = np.maximum(n, 1).astype(np.float32)
    large = exact + (np.log(nf / exact) / math.log(REL_MAX_DIST / exact)
                     * (REL_BUCKETS - exact)).astype(np.int32)
    return np.where(n < exact, n, np.minimum(large, REL_BUCKETS - 1)).astype(np.int32)


def _bias_tiles(rel_table, rel):
    bucket = jnp.asarray(_t5_bucket_np(rel))
    return pl.pallas_call(
        _bias_kernel,
        in_specs=[pl.BlockSpec(memory_space=pltpu.SMEM), pl.BlockSpec(memory_space=pltpu.VMEM)],
        out_specs=pl.BlockSpec(memory_space=pltpu.VMEM),
        out_shape=jax.ShapeDtypeStruct((B_HEADS,) + rel.shape, F32),
    )(rel_table, bucket)


def _out_proj_kernel(x_ref, ma_ref, mb_ref, p_ref, wout_ref, gple_ref, wgate_ref, wproj_ref, gfin_ref,
                     y_ref, x1_scr, hn_scr, *, cw):
    d = x_ref.shape[1]
    nc = d // cw
    half = ma_ref.shape[1]
    ss = jnp.zeros((x_ref.shape[0], 1), F32)
    for c in range(nc):
        cols = slice(c * cw, (c + 1) * cw)
        x1 = (x_ref[:, cols] + _dot(ma_ref[...], wout_ref[:half, cols])
              + _dot(mb_ref[...], wout_ref[half:, cols]))
        x1_scr[:, cols] = x1
        ss = ss + jnp.sum(x1 * x1, axis=-1, keepdims=True)
    rs = lax.rsqrt(ss / d + EPS)
    for c in range(nc):
        cols = slice(c * cw, (c + 1) * cw)
        hn_scr[:, cols] = (x1_scr[:, cols] * rs * gple_ref[:, cols]).astype(BF16)
    pb = p_ref[...].astype(BF16)
    ss = jnp.zeros((x_ref.shape[0], 1), F32)
    for c in range(nc):
        cols = slice(c * cw, (c + 1) * cw)
        gt = 1.0 / (1.0 + jnp.exp(-_dot(hn_scr[...], wgate_ref[:, cols])))
        x2 = x1_scr[:, cols] + gt * _dot(pb, wproj_ref[:, cols])
        x1_scr[:, cols] = x2
        ss = ss + jnp.sum(x2 * x2, axis=-1, keepdims=True)
    rs = lax.rsqrt(ss / d + EPS)
    for c in range(nc):
        cols = slice(c * cw, (c + 1) * cw)
        y_ref[:, cols] = x1_scr[:, cols] * rs * gfin_ref[:, cols]


def _out_proj(x, ma, mb, p, w_out, g_ple, w_gate, w_proj, g_final, tm):
    n, d = x.shape
    row = lambda w: pl.BlockSpec((tm, w), lambda i: (i, 0))
    return pl.pallas_call(
        functools.partial(_out_proj_kernel, cw=min(512, d)),
        grid=(n // tm,),
        in_specs=[row(d), row(ma.shape[1]), row(mb.shape[1]), row(p.shape[1]),
                  _resident(w_out.shape), _resident(g_ple.shape), _resident(w_gate.shape),
                  _resident(w_proj.shape), _resident(g_final.shape)],
        out_specs=row(d),
        out_shape=jax.ShapeDtypeStruct((n, d), F32),
        scratch_shapes=[pltpu.VMEM((tm, d), F32), pltpu.VMEM((tm, d), BF16)],
        compiler_params=_params(("parallel",)),
    )(x, ma, mb, p, w_out, g_ple, w_gate, w_proj, g_final)


def _page_copies(pt_ref, pools, bufs, lane_major, sems, q, chunk, slot, pages_per_chunk):
    copies = []
    for k, (pool, buf, lm) in enumerate(zip(pools, bufs, lane_major)):
        rows, width = pool.shape[2], pool.shape[3]
        for p in range(pages_per_chunk):
            pid = pt_ref[q, chunk * pages_per_chunk + p]
            if lm:
                dst = buf.at[slot, :, pl.ds(p * width, width)]
            else:
                dst = buf.at[slot, pl.ds(p * rows, rows)]
            copies.append(pltpu.make_async_copy(pool.at[0, pid], dst, sems.at[slot, k]))
    return copies


def _sample_pass1_kernel(pt_ref, qlat_ref, qpe_ref, qidx_ref, wi_ref, ckvn_ref, kpen_ref, kidxn_ref,
                         ga_ref, wuv_ref, pool_ckv, pool_kpe, pool_kidx,
                         out_ref, score_ref,
                         buf_ckv, buf_kpe, buf_kidx, sems, m_scr, l_scr, acc_scr,
                         *, n_chunks, pages_per_chunk, page, past):
    g = pl.program_id(0)
    n_steps = pl.num_programs(0)
    q = g // n_chunks
    c = g % n_chunks
    slot = g % 2
    pools = (pool_ckv, pool_kpe, pool_kidx)
    bufs = (buf_ckv, buf_kpe, buf_kidx)
    ck = pages_per_chunk * page

    def copies(step, slot_):
        return _page_copies(pt_ref, pools, bufs, (False, True, True), sems, step // n_chunks,
                            step % n_chunks, slot_, pages_per_chunk)

    def wait_slot(slot_):
        for k, buf in enumerate(bufs):
            pltpu.make_async_copy(buf.at[slot_], buf.at[slot_], sems.at[slot_, k]).wait()

    @pl.when(g == 0)
    def _():
        for cp in copies(g, slot):
            cp.start()

    qlat = qlat_ref[0].astype(F32)
    qpe = qpe_ref[0].astype(F32)
    qidx = qidx_ref[0].astype(F32)
    wcol = wi_ref[0]

    @pl.when(c == 0)
    def _():
        ckvn = ckvn_ref[0]
        s_new = (jnp.sum(qlat * ckvn, axis=-1, keepdims=True)
                 + jnp.sum(qpe * kpen_ref[0], axis=-1, keepdims=True)) * MLA_SCALE
        m_scr[...] = s_new
        l_scr[...] = jnp.ones_like(s_new)
        acc_scr[...] = jnp.broadcast_to(ckvn, acc_scr.shape)
        si = jnp.maximum(jnp.sum(qidx * kidxn_ref[0], axis=-1, keepdims=True), 0.0) * wcol
        si = jnp.sum(si, axis=0, keepdims=True)
        lane = lax.broadcasted_iota(I32, (1, SCORE_PAD), 1)
        score_ref[0, :, past:] = jnp.where(lane == 0, si, -jnp.inf)

    wait_slot(slot)
    for cp in copies(jnp.minimum(g + 1, n_steps - 1), 1 - slot):
        cp.start()

    kc = buf_ckv[slot]
    s = (_dot_nt(qlat, kc) + _dot(qpe, buf_kpe[slot])) * MLA_SCALE
    m, l, acc = _flash_update((m_scr[...], l_scr[...], acc_scr[...]), s, None, kc)
    m_scr[...] = m
    l_scr[...] = l
    acc_scr[...] = acc

    si = jnp.maximum(_dot(qidx, buf_kidx[slot]), 0.0) * wcol
    score_ref[0, :, pl.ds(pl.multiple_of(c * ck, ck), ck)] = jnp.sum(si, axis=0, keepdims=True)

    @pl.when(c == n_chunks - 1)
    def _():
        lat = (acc / l).astype(BF16)
        for hh in range(A_HEADS):
            o = _dot(lat, wuv_ref[hh])[hh:hh + 1]
            cols = slice(hh * A_V, (hh + 1) * A_V)
            out_ref[0, :, cols] = o * _silu(ga_ref[0, :, cols])

    @pl.when(g == n_steps - 1)
    def _():
        wait_slot(1 - slot)


def _sample_pass1(page_table, qlat, qpe, qidx, wi, ckvn, kpen, kidxn, gate_a, w_uv,
                  pool_ckv, pool_kpe, pool_kidx, pages_per_chunk):
    nq, n_pages = page_table.shape
    page = pool_ckv.shape[2]
    past = n_pages * page
    n_chunks = n_pages // pages_per_chunk
    ck = pages_per_chunk * page
    per_q = lambda a: pl.BlockSpec((1,) + a.shape[1:], lambda g, pt: (g // n_chunks, 0, 0))
    any_spec = pl.BlockSpec(memory_space=pl.ANY)
    grid_spec = pltpu.PrefetchScalarGridSpec(
        num_scalar_prefetch=1,
        grid=(nq * n_chunks,),
        in_specs=[per_q(qlat), per_q(qpe), per_q(qidx), per_q(wi), per_q(ckvn), per_q(kpen),
                  per_q(kidxn), per_q(gate_a),
                  pl.BlockSpec(w_uv.shape, lambda g, pt: (0, 0, 0), pipeline_mode=pl.Buffered(1)),
                  any_spec, any_spec, any_spec],
        out_specs=[pl.BlockSpec((1, 1, A_WIDTH), lambda g, pt: (g // n_chunks, 0, 0)),
                   pl.BlockSpec((1, 1, past + SCORE_PAD), lambda g, pt: (g // n_chunks, 0, 0))],
        scratch_shapes=[pltpu.VMEM((2, ck, KV_LORA), F32), pltpu.VMEM((2, A_ROPE, ck), F32),
                        pltpu.VMEM((2, IDX_DIM, ck), F32), pltpu.SemaphoreType.DMA((2, 3)),
                        pltpu.VMEM((A_HEADS, 1), F32), pltpu.VMEM((A_HEADS, 1), F32),
                        pltpu.VMEM((A_HEADS, KV_LORA), F32)],
    )
    return pl.pallas_call(
        functools.partial(_sample_pass1_kernel, n_chunks=n_chunks, pages_per_chunk=pages_per_chunk,
                          page=page, past=past),
        grid_spec=grid_spec,
        out_shape=[jax.ShapeDtypeStruct((nq, 1, A_WIDTH), F32),
                   jax.ShapeDtypeStruct((nq, 1, past + SCORE_PAD), F32)],
        compiler_params=_params(("arbitrary",)),
    )(page_table, qlat, qpe, qidx, wi, ckvn, kpen, kidxn, gate_a, w_uv, pool_ckv, pool_kpe, pool_kidx)


def _topk_thresh_kernel(score_ref, thr_ref, *, k_top):
    key = _sort_key(score_ref[...])

    def count_ge(cand):
        ge = (key >= cand).astype(F32)
        return jnp.sum(jnp.sum(ge, axis=2, keepdims=True), axis=1, keepdims=True)

    thr = _kth_largest_key(count_ge, (key.shape[0], 1, 1), k_top)
    thr_ref[...] = jnp.broadcast_to(thr, thr_ref.shape)


def _topk_thresh(score3, k_top):
    nq = score3.shape[0]
    return pl.pallas_call(
        functools.partial(_topk_thresh_kernel, k_top=k_top),
        in_specs=[pl.BlockSpec(memory_space=pltpu.VMEM)],
        out_specs=pl.BlockSpec(memory_space=pltpu.VMEM),
        out_shape=jax.ShapeDtypeStruct((nq, 1, LANES), I32),
        compiler_params=_params(),
    )(score3)


ROW_TILE = 8
SLOTS_PER_TILE = ROW_TILE // B_KV_HEADS


def _select_rows_kernel(score_ref, thr_ref, pt_ref, tix_ref, sub_ref, nsel_ref, *, n_pages, page, n_rows):
    key = _sort_key(score_ref[0, :n_pages, :])
    prow = lax.broadcasted_iota(I32, (n_pages, page), 0)
    m = ((key >= thr_ref[0][:, :1]) & (prow < n_pages - 1)).astype(F32)
    r_i = lax.broadcasted_iota(I32, (n_pages, n_pages), 0)
    c_i = lax.broadcasted_iota(I32, (n_pages, n_pages), 1)
    cnt = _dot(m.astype(BF16), jnp.ones((page, LANES), BF16))
    o_excl = _dot((c_i < r_i).astype(BF16), cnt.astype(BF16))
    o_incl = o_excl + cnt
    reps = n_rows // LANES
    o_ex = jnp.concatenate([o_excl] * reps, axis=1)
    o_in = jnp.concatenate([o_incl] * reps, axis=1)
    jrow = lax.broadcasted_iota(I32, (n_pages, n_rows), 1).astype(F32)
    hit = ((o_ex <= jrow) & (jrow < o_in)).astype(F32)
    pt_j = jnp.sum(hit * pt_ref[0], axis=0, keepdims=True)
    o_j = jnp.sum(hit * o_ex, axis=0, keepdims=True)
    used = jnp.sum(hit, axis=0, keepdims=True)
    rank = jrow[:1] - o_j
    s_r = lax.broadcasted_iota(I32, (page, page), 0)
    s_c = lax.broadcasted_iota(I32, (page, page), 1)
    pst = _dot((s_c <= s_r).astype(BF16), m.T.astype(BF16))
    psj = _dot(pst.astype(BF16), hit.astype(BF16))
    slot_j = jnp.sum((psj <= rank).astype(F32), axis=0, keepdims=True)
    quad = jnp.floor(slot_j * (1.0 / SLOTS_PER_TILE))
    tile = pt_j * (page // SLOTS_PER_TILE) + quad
    tix_ref[0] = jnp.where(used > 0.0, tile, 0.0).astype(I32)
    sub_ref[0] = ((slot_j - SLOTS_PER_TILE * quad) * B_KV_HEADS).astype(I32)
    nsel_ref[0] = o_incl[n_pages - 1:n_pages, :].astype(I32)


def _select_rows(score3, thr, pt_col, n_rows):
    nq, rows, page = score3.shape
    n_pages = pt_col.shape[1]
    assert n_rows % LANES == 0 and page == LANES and n_pages % 8 == 0
    per_q = lambda a: pl.BlockSpec((1,) + a.shape[1:], lambda q: (q, 0, 0))
    out = lambda w: pl.BlockSpec((1, 1, w), lambda q: (q, 0, 0))
    return pl.pallas_call(
        functools.partial(_select_rows_kernel, n_pages=n_pages, page=page, n_rows=n_rows),
        grid=(nq,),
        in_specs=[per_q(score3), per_q(thr), per_q(pt_col)],
        out_specs=[out(n_rows), out(n_rows), out(LANES)],
        out_shape=[jax.ShapeDtypeStruct((nq, 1, n_rows), I32), jax.ShapeDtypeStruct((nq, 1, n_rows), I32),
                   jax.ShapeDtypeStruct((nq, 1, LANES), I32)],
        compiler_params=_params(("parallel",)),
    )(score3, thr, pt_col)


def _sample_pass2_kernel(pt_ref, tix_ref, cfar_ref, qb_ref, kbn_ref, vbn_ref, gb_ref, sub_ref, nsel_ref,
                         slast_ref, snew_ref, thr_ref, tail_ref, bnew_ref, pool_k, pool_v, out_ref,
                         buf_k, buf_v, sems, *, n_pages, page, n_rows):
    q = pl.program_id(0)
    nq = pl.num_programs(0)
    slot = q % 2
    gw = ROW_TILE * n_rows
    lw = B_KV_HEADS * page
    pools_bufs = ((pool_k, buf_k), (pool_v, buf_v))

    def start_all(qq, slot_):
        for k, (pool, buf) in enumerate(pools_bufs):
            for j in range(n_rows):
                off = pl.multiple_of(tix_ref[qq, j] * ROW_TILE, ROW_TILE)
                pltpu.make_async_copy(pool.at[pl.ds(off, ROW_TILE)],
                                      buf.at[slot_, pl.ds(j * ROW_TILE, ROW_TILE)], sems.at[slot_, k]).start()
            last = pl.multiple_of(pt_ref[qq, n_pages - 1] * lw, lw)
            pltpu.make_async_copy(pool.at[pl.ds(last, lw)], buf.at[slot_, pl.ds(gw, lw)],
                                  sems.at[slot_, k]).start()

    @pl.when(q == 0)
    def _():
        start_all(q, slot)

    @pl.when(q + 1 < nq)
    def _():
        start_all(q + 1, 1 - slot)

    qb = qb_ref[0].astype(F32)
    thr = thr_ref[0][:, :1]
    kv_of_row = (lax.broadcasted_iota(I32, (B_HEADS, 1), 0) >= B_GROUP).astype(I32)
    head0 = kv_of_row == 0
    lo, hi = slice(0, B_HD), slice(B_HD, 2 * B_HD)

    keep = _sort_key(snew_ref[0, :, :LANES])[:, :1] >= thr
    kbn = kbn_ref[0]
    vbn = vbn_ref[0]
    s_new = jnp.where(head0, jnp.sum(qb * kbn[:, lo], axis=-1, keepdims=True),
                      jnp.sum(qb * kbn[:, hi], axis=-1, keepdims=True))
    s_new = jnp.where(keep, s_new * B_SCALE + bnew_ref[...], NEG)
    v_new = jnp.where(head0, jnp.broadcast_to(vbn[:, lo], (B_HEADS, B_HD)),
                      jnp.broadcast_to(vbn[:, hi], (B_HEADS, B_HD)))

    lane_g = lax.broadcasted_iota(I32, (B_HEADS, gw), 1)
    valid_g = ((lane_g & (ROW_TILE - 1)) == sub_ref[0] + kv_of_row) & (lane_g < ROW_TILE * nsel_ref[0][:, :1])
    lane_l = lax.broadcasted_iota(I32, (B_HEADS, lw), 1)
    sel_l = _sort_key(slast_ref[0]) >= thr
    valid_l = jnp.broadcast_to(sel_l, (B_HEADS, lw)) & ((lane_l & 1) == kv_of_row)
    valid = jnp.concatenate([valid_g, valid_l], axis=1)
    bias = jnp.concatenate([jnp.broadcast_to(cfar_ref[...], (B_HEADS, gw)), tail_ref[...]], axis=1)

    for k, (pool, buf) in enumerate(pools_bufs):
        pltpu.make_async_copy(pool.at[pl.ds(0, gw + lw)], buf.at[slot], sems.at[slot, k]).wait()

    s = _dot_nt(qb, buf_k[slot]) * B_SCALE + bias
    s = jnp.where(valid, s, NEG)
    m = jnp.maximum(jnp.max(s, axis=-1, keepdims=True), s_new)
    p = jnp.where(valid, jnp.exp(s - m), 0.0)
    p_new = jnp.where(keep, jnp.exp(s_new - m), 0.0)
    l = jnp.sum(p, axis=-1, keepdims=True) + p_new
    o = (_dot(p, buf_v[slot]) + p_new * v_new) / l
    for hh in range(B_HEADS):
        cols = slice(hh * B_HD, (hh + 1) * B_HD)
        out_ref[0, :, cols] = o[hh:hh + 1] * _silu(gb_ref[0, :, cols])


def _sample_pass2(page_table, tix, cfar, qb, kbn, vbn, gate_b, sub_rep, nsel, slast2, score, thr, tail2,
                  bias_new, pool_k2, pool_v2, page):
    nq, n_pages = page_table.shape
    n_rows = tix.shape[1]
    past = n_pages * page
    rows = ROW_TILE * n_rows + B_KV_HEADS * page
    assert past % SCORE_PAD == 0
    per_q = lambda a: pl.BlockSpec((1,) + a.shape[1:], lambda q, pt, tx: (q, 0, 0))
    const2 = lambda a: pl.BlockSpec(a.shape, lambda q, pt, tx: (0, 0))
    any_spec = pl.BlockSpec(memory_space=pl.ANY)
    grid_spec = pltpu.PrefetchScalarGridSpec(
        num_scalar_prefetch=2,
        grid=(nq,),
        in_specs=[const2(cfar), per_q(qb), per_q(kbn), per_q(vbn), per_q(gate_b), per_q(sub_rep),
                  per_q(nsel), per_q(slast2),
                  pl.BlockSpec((1, 1, SCORE_PAD), lambda q, pt, tx: (q, 0, past // SCORE_PAD)),
                  per_q(thr), const2(tail2), const2(bias_new), any_spec, any_spec],
        out_specs=pl.BlockSpec((1, 1, B_WIDTH), lambda q, pt, tx: (q, 0, 0)),
        scratch_shapes=[pltpu.VMEM((2, rows, B_HD), F32), pltpu.VMEM((2, rows, B_HD), F32),
                        pltpu.SemaphoreType.DMA((2, 2))],
    )
    return pl.pallas_call(
        functools.partial(_sample_pass2_kernel, n_pages=n_pages, page=page, n_rows=n_rows),
        grid_spec=grid_spec,
        out_shape=jax.ShapeDtypeStruct((nq, 1, B_WIDTH), F32),
        compiler_params=_params(("arbitrary",)),
    )(page_table, tix, cfar, qb, kbn, vbn, gate_b, sub_rep, nsel, slast2, score, thr, tail2, bias_new,
      pool_k2, pool_v2)


def _prep_w_in(w_in):
    offs = np.concatenate([[0], np.cumsum(IN_SIZES)])
    part = lambda k: w_in[:, offs[k]:offs[k + 1]]
    c_q, c_kv, k_pe, gate_a, q_b, k_b, v_b, gate_b, q_idx, k_idx, w_idx = [part(k) for k in range(11)]
    pad = jnp.zeros((w_in.shape[0], LANES - IDX_HEADS), w_in.dtype)
    cols = [c_q, c_kv, gate_a, q_b, k_b, v_b, gate_b, q_idx, k_pe, k_idx, w_idx, pad]
    return jnp.concatenate(cols, axis=1).astype(BF16)


def _prep_w_uq(w_uq):
    w = w_uq.reshape(w_uq.shape[0], A_HEADS, A_NOPE + A_ROPE)
    nope = w[:, :, :A_NOPE].reshape(w_uq.shape[0], A_HEADS * A_NOPE)
    rope = w[:, :, A_NOPE:].reshape(w_uq.shape[0], A_HEADS * A_ROPE)
    return jnp.concatenate([nope, rope], axis=1).astype(BF16)


def kernel(x_prompt, x_sample, p_prompt, p_sample, cache_ckv, cache_kpe, cache_k, cache_v, cache_kidx,
           page_table, rel_table, g_attn, w_in, g_q, w_uq, g_kv, w_uk, w_uv, w_out, g_ple, w_ple_gate,
           w_ple_proj, g_final):
    nb, seq, d = x_prompt.shape
    nq, nt, _ = x_sample.shape
    depth = w_in.shape[0]
    assert depth == 1 and nt == 1
    n_pages = page_table.shape[1]
    page = cache_ckv.shape[2]
    past = n_pages * page
    tq = min(256, seq)
    tm_s = min(128, nq)
    pages_per_chunk = min(32, n_pages)
    assert seq % tq == 0 and nq % tm_s == 0 and n_pages % pages_per_chunk == 0
    assert tq + 1 > REL_MAX_DIST

    row2 = lambda v: v.reshape(1, -1)
    w_in_p = _prep_w_in(w_in[0])
    w_uq_p = _prep_w_uq(w_uq[0])
    w_uk_b = w_uk[0].astype(BF16)
    w_uv_b = w_uv[0].astype(BF16)
    w_out_b = w_out[0].astype(BF16)
    w_gate_b = w_ple_gate[0].astype(BF16)
    w_proj_b = w_ple_proj[0].astype(BF16)
    cfar = rel_table[REL_BUCKETS - 1]

    tabs_p = _rope_tables(jnp.arange(seq, dtype=I32))
    xp = x_prompt.reshape(nb * seq, d)
    (qcat, qi2, _, _, wi, qb, ga, gb, ckv, kpe, kb, vb, kidx, kcat, ii, kbb, vbb) = _proj_in(
        xp, tabs_p, seq // tq, row2(g_attn[0]), w_in_p, row2(g_q[0]), w_uq_p, row2(g_kv[0]), w_uk_b, tq)
    b3 = lambda a: a.reshape((nb, seq) + a.shape[1:])
    b4 = lambda a: a.reshape((a.shape[0], nb, seq) + a.shape[2:])
    ma = _mla_prompt(b4(qcat), b3(kcat), b3(ga), w_uv_b, tq)
    rel = np.arange(tq)[:, None] + tq - np.arange(2 * tq)[None, :]
    bias = _bias_tiles(rel_table, rel)
    k_top = min(IDX_TOPK, seq // 4)
    mb = _dsa_prompt(cfar, b4(qi2), b3(wi), b3(ii), b3(qb), b3(kbb), b3(vbb), b3(gb), bias, tq, k_top)
    y_prompt = _out_proj(xp, ma.reshape(nb * seq, -1), mb.reshape(nb * seq, -1),
                         p_prompt[0].reshape(nb * seq, -1), w_out_b, row2(g_ple[0]), w_gate_b, w_proj_b,
                         row2(g_final), tq).reshape(nb, seq, d)

    tabs_s = _rope_tables(jnp.full((tm_s,), past, I32))
    xs = x_sample.reshape(nq, d)
    (qcat_s, _, qpe_s, qidx_s, wi_s, qb_s, ga_s, gb_s, ckv_s, kpe_s, kb_s, vb_s, kidx_s, _, _, _, _) = _proj_in(
        xs, tabs_s, 1, row2(g_attn[0]), w_in_p, row2(g_q[0]), w_uq_p, row2(g_kv[0]), w_uk_b, tm_s)
    qlat_s = jnp.swapaxes(qcat_s[:, :, :KV_LORA], 0, 1)
    ma_s, score = _sample_pass1(
        page_table, qlat_s, qpe_s.reshape(nq, A_HEADS, A_ROPE), qidx_s.reshape(nq, IDX_HEADS, IDX_DIM),
        wi_s[:, :IDX_HEADS].reshape(nq, IDX_HEADS, 1), ckv_s.reshape(nq, 1, -1), kpe_s.reshape(nq, 1, -1),
        kidx_s.reshape(nq, 1, -1), ga_s.reshape(nq, 1, -1), w_uv_b, cache_ckv,
        jnp.swapaxes(cache_kpe, 2, 3), jnp.swapaxes(cache_kidx, 2, 3), pages_per_chunk)
    k_top_s = min(IDX_TOPK, (past + nt) // 4)
    score3 = score.reshape(nq, n_pages + SCORE_PAD // LANES, LANES)
    thr = _topk_thresh(score3, k_top_s)
    tix, sub, nsel = _select_rows(score3, thr, page_table.astype(F32).reshape(nq, n_pages, 1), k_top_s)
    rel_tail = np.broadcast_to(np.concatenate(
        [np.repeat(page - np.arange(page), 2), np.zeros(LANES, np.int64)])[None, :], (8, 2 * page + LANES))
    tail = _bias_tiles(rel_table, rel_tail)[:, 0, :]
    rows2 = lambda pool: pool.reshape(pool.shape[1] * B_KV_HEADS * page, B_HD)
    mb_s = _sample_pass2(page_table, tix.reshape(nq, -1), cfar.reshape(B_HEADS, 1),
                         qb_s.reshape(nq, B_HEADS, B_HD), kb_s.reshape(nq, 1, -1), vb_s.reshape(nq, 1, -1),
                         gb_s.reshape(nq, 1, -1), jnp.repeat(sub, ROW_TILE, axis=2), nsel,
                         jnp.repeat(score[:, :, past - page:past], 2, axis=2), score, thr,
                         tail[:, :2 * page], tail[:, 2 * page:2 * page + 1],
                         rows2(cache_k), rows2(cache_v), page)
    y_sample = _out_proj(xs, ma_s.reshape(nq, -1).astype(BF16), mb_s.reshape(nq, -1).astype(BF16),
                         p_sample[0].reshape(nq, -1), w_out_b, row2(g_ple[0]), w_gate_b, w_proj_b,
                         row2(g_final), tm_s).reshape(nq, nt, d)

    return (y_prompt, y_sample,
            ckv.reshape(1, nb, seq, -1), kpe.reshape(1, nb, seq, -1),
            kb.reshape(1, nb, seq, B_KV_HEADS, B_HD), vb.reshape(1, nb, seq, B_KV_HEADS, B_HD),
            kidx.reshape(1, nb, seq, -1),
            ckv_s.reshape(1, nq, nt, -1), kpe_s.reshape(1, nq, nt, -1),
            kb_s.reshape(1, nq, nt, B_KV_HEADS, B_HD), vb_s.reshape(1, nq, nt, B_KV_HEADS, B_HD),
            kidx_s.reshape(1, nq, nt, -1))
```

```python
import functools
import math

import numpy as np
import jax
import jax.numpy as jnp
from jax import lax
from jax.experimental import pallas as pl
from jax.experimental.pallas import tpu as pltpu

F32 = jnp.float32
BF16 = jnp.bfloat16
I32 = jnp.int32

A_HEADS = 8
A_NOPE = 128
A_ROPE = 64
A_V = 128
A_WIDTH = A_HEADS * A_V
Q_LORA = 512
KV_LORA = 256
B_HEADS = 8
B_KV_HEADS = 2
B_HD = 128
B_GROUP = B_HEADS // B_KV_HEADS
B_WIDTH = B_HEADS * B_HD
B_KVW = B_KV_HEADS * B_HD
IDX_HEADS = 16
IDX_DIM = 64
IDX_ROPE = 32
IDX_TOPK = 256
REL_BUCKETS = 32
REL_MAX_DIST = 128
ROPE_THETA = 10000.0
EPS = 1e-6
MLA_SCALE = (A_NOPE + A_ROPE) ** -0.5
B_SCALE = B_HD ** -0.5
IDX_W_SCALE = (IDX_HEADS ** -0.5) * (IDX_DIM ** -0.5)
IN_SIZES = (Q_LORA, KV_LORA, A_ROPE, A_WIDTH, B_WIDTH, B_KVW, B_KVW, B_WIDTH,
            IDX_HEADS * IDX_DIM, IDX_DIM, IDX_HEADS)

LANES = 128
NEG = -1e30
VMEM_LIMIT = 56 * 1024 * 1024
INT_MIN = -2 ** 31

C_Q = 0
C_KV = C_Q + Q_LORA
C_GA = C_KV + KV_LORA
C_QB = C_GA + A_WIDTH
C_KB = C_QB + B_WIDTH
C_VB = C_KB + B_KVW
C_GB = C_VB + B_KVW
C_QI = C_GB + B_WIDTH
C_SLAB = C_QI + IDX_HEADS * IDX_DIM
C_WI = C_SLAB + LANES
NP_IN = C_WI + LANES
QCAT = KV_LORA + LANES
N_TAB = 11
SCORE_PAD = 8 * LANES


def _dot(a, b):
    return jnp.dot(a, b, preferred_element_type=F32)


def _dot_nt(a, b):
    return lax.dot_general(a, b, (((1,), (1,)), ((), ())), preferred_element_type=F32)


def _silu(x):
    return x / (1.0 + jnp.exp(-x))


def _params(sem=None):
    return pltpu.CompilerParams(dimension_semantics=sem, vmem_limit_bytes=VMEM_LIMIT)


def _resident(shape):
    nd = len(shape)
    return pl.BlockSpec(shape, lambda *_: (0,) * nd, pipeline_mode=pl.Buffered(1))


def _rope_tables(pos):
    def cs(half):
        inv = ROPE_THETA ** (-jnp.arange(half, dtype=F32) / half)
        ang = pos.astype(F32)[:, None] * inv[None, :]
        return jnp.cos(ang), jnp.sin(ang)
    n = pos.shape[0]
    c32, s32 = cs(A_ROPE // 2)
    c16, s16 = cs(IDX_ROPE // 2)
    z16 = jnp.zeros((n, 16), F32)
    z32 = jnp.zeros((n, 32), F32)
    z64 = jnp.zeros((n, 64), F32)
    o32 = jnp.ones((n, 32), F32)
    cat = lambda *a: jnp.concatenate(a, axis=1)
    a64, bm64, bp64 = cat(c32, c32), cat(-s32, z32), cat(z32, s32)
    a16, cm16, cp16 = cat(c16, c16, o32), cat(-s16, z16, z32), cat(z16, s16, z32)
    tabs = [cat(a64, a64), cat(bm64, bm64), cat(bp64, bp64),
            cat(a16, a16), cat(cm16, cm16), cat(cp16, cp16),
            cat(a64, a16), cat(bm64, z64), cat(bp64, z64), cat(z64, cm16), cat(z64, cp16)]
    return jnp.concatenate(tabs, axis=1)


def _proj_in_kernel(x_ref, gattn_ref, win_ref, gq_ref, wuq_ref, gkv_ref, wuk_ref, tab_ref,
                    qcat_ref, qi2_ref, qpe_ref, qidx_ref, wi_ref, qb_ref, ga_ref, gb_ref,
                    ckv_ref, kpe_ref, kb_ref, vb_ref, kidx_ref, kcat_ref, ii_ref, kbb_ref, vbb_ref,
                    h_scr):
    x = x_ref[...]
    ms = jnp.mean(x * x, axis=-1, keepdims=True)
    h_scr[...] = (x * lax.rsqrt(ms + EPS) * gattn_ref[...]).astype(BF16)

    def zcols(c0, width):
        return _dot(h_scr[...], win_ref[:, c0:c0 + width])

    def tab(k):
        return tab_ref[:, k * LANES:(k + 1) * LANES]

    tm = x.shape[0]
    lane = lax.broadcasted_iota(I32, (tm, LANES), 1)
    lo = lane < 64

    def rope(xg, a, bm, bp, shift):
        return (xg * tab(a) + pltpu.roll(xg, LANES - shift, 1) * tab(bm)
                + pltpu.roll(xg, shift, 1) * tab(bp))

    cq = zcols(C_Q, Q_LORA)
    cq = cq * lax.rsqrt(jnp.mean(cq * cq, axis=-1, keepdims=True) + EPS) * gq_ref[...]
    q = _dot(cq.astype(BF16), wuq_ref[...])
    n_nope = A_HEADS * A_NOPE
    for g in range(A_HEADS // 2):
        xg = q[:, n_nope + g * LANES:n_nope + (g + 1) * LANES]
        rg = rope(xg, 0, 1, 2, A_ROPE // 2)
        qpe_ref[:, g * LANES:(g + 1) * LANES] = rg.astype(BF16)
        qcat_ref[2 * g, :, KV_LORA:] = jnp.where(lo, rg, 0.0).astype(BF16)
        qcat_ref[2 * g + 1, :, KV_LORA:] = jnp.where(lo, 0.0, rg).astype(BF16)
    for hh in range(A_HEADS):
        qn = q[:, hh * A_NOPE:(hh + 1) * A_NOPE].astype(BF16)
        qcat_ref[hh, :, :KV_LORA] = _dot(qn, wuk_ref[hh]).astype(BF16)

    ckv = zcols(C_KV, KV_LORA)
    ckv = ckv * lax.rsqrt(jnp.mean(ckv * ckv, axis=-1, keepdims=True) + EPS) * gkv_ref[...]
    ckv_ref[...] = ckv
    kcat_ref[:, :KV_LORA] = ckv.astype(BF16)

    slab = zcols(C_SLAB, LANES)
    slab = (slab * tab(6) + pltpu.roll(slab, LANES - 32, 1) * tab(7) + pltpu.roll(slab, 32, 1) * tab(8)
            + pltpu.roll(slab, LANES - 16, 1) * tab(9) + pltpu.roll(slab, 16, 1) * tab(10))
    swapped = pltpu.roll(slab, 64, 1)
    kpe_ref[...] = slab[:, :A_ROPE]
    kidx_ref[...] = swapped[:, :IDX_DIM]
    kcat_ref[:, KV_LORA:] = jnp.where(lo, slab, swapped).astype(BF16)
    ii_ref[...] = jnp.where(lo, swapped, slab).astype(BF16)

    kb = zcols(C_KB, B_KVW)
    kb_ref[...] = kb
    kbb_ref[...] = kb.astype(BF16)
    vb = zcols(C_VB, B_KVW)
    vb_ref[...] = vb
    vbb_ref[...] = vb.astype(BF16)
    for c in range(B_WIDTH // 512):
        qb_ref[:, c * 512:(c + 1) * 512] = zcols(C_QB + c * 512, 512).astype(BF16)
        ga_ref[:, c * 512:(c + 1) * 512] = zcols(C_GA + c * 512, 512)
        gb_ref[:, c * 512:(c + 1) * 512] = zcols(C_GB + c * 512, 512)

    for c in range(IDX_HEADS * IDX_DIM // 512):
        qi = zcols(C_QI + c * 512, 512)
        for g in range(4):
            rg = rope(qi[:, g * LANES:(g + 1) * LANES], 3, 4, 5, IDX_ROPE // 2)
            col = c * 512 + g * LANES
            qidx_ref[:, col:col + LANES] = rg.astype(BF16)
            hh = col // IDX_DIM
            qi2_ref[hh] = jnp.where(lo, rg, 0.0).astype(BF16)
            qi2_ref[hh + 1] = jnp.where(lo, 0.0, rg).astype(BF16)
    wi_ref[...] = zcols(C_WI, LANES) * IDX_W_SCALE


def _proj_in(x, tabs, tab_blocks, g_attn, w_in_p, g_q, w_uq_p, g_kv, w_uk, tm):
    n, d = x.shape
    grid = (n // tm,)
    row = lambda w: pl.BlockSpec((tm, w), lambda i: (i, 0))
    head = lambda nh, w: pl.BlockSpec((nh, tm, w), lambda i: (0, i, 0))
    out_shape = [
        jax.ShapeDtypeStruct((A_HEADS, n, QCAT), BF16),
        jax.ShapeDtypeStruct((IDX_HEADS, n, LANES), BF16),
        jax.ShapeDtypeStruct((n, A_HEADS * A_ROPE), BF16),
        jax.ShapeDtypeStruct((n, IDX_HEADS * IDX_DIM), BF16),
        jax.ShapeDtypeStruct((n, LANES), F32),
        jax.ShapeDtypeStruct((n, B_WIDTH), BF16),
        jax.ShapeDtypeStruct((n, A_WIDTH), F32),
        jax.ShapeDtypeStruct((n, B_WIDTH), F32),
        jax.ShapeDtypeStruct((n, KV_LORA), F32),
        jax.ShapeDtypeStruct((n, A_ROPE), F32),
        jax.ShapeDtypeStruct((n, B_KVW), F32),
        jax.ShapeDtypeStruct((n, B_KVW), F32),
        jax.ShapeDtypeStruct((n, IDX_DIM), F32),
        jax.ShapeDtypeStruct((n, QCAT), BF16),
        jax.ShapeDtypeStruct((n, LANES), BF16),
        jax.ShapeDtypeStruct((n, B_KVW), BF16),
        jax.ShapeDtypeStruct((n, B_KVW), BF16),
    ]
    out_specs = [head(A_HEADS, QCAT), head(IDX_HEADS, LANES), row(A_HEADS * A_ROPE),
                 row(IDX_HEADS * IDX_DIM), row(LANES), row(B_WIDTH), row(A_WIDTH), row(B_WIDTH),
                 row(KV_LORA), row(A_ROPE), row(B_KVW), row(B_KVW), row(IDX_DIM),
                 row(QCAT), row(LANES), row(B_KVW), row(B_KVW)]
    in_specs = [row(d), _resident(g_attn.shape), _resident(w_in_p.shape), _resident(g_q.shape),
                _resident(w_uq_p.shape), _resident(g_kv.shape), _resident(w_uk.shape),
                pl.BlockSpec((tm, N_TAB * LANES), lambda i: (i % tab_blocks, 0))]
    return pl.pallas_call(
        _proj_in_kernel, grid=grid, in_specs=in_specs, out_specs=out_specs, out_shape=out_shape,
        scratch_shapes=[pltpu.VMEM((tm, d), BF16)],
        compiler_params=_params(("parallel",)),
    )(x, g_attn, w_in_p, g_q, w_uq_p, g_kv, w_uk, tabs)


def _flash_update(carry, s, valid, v):
    m, l, acc = carry
    if valid is not None:
        s = jnp.where(valid, s, NEG)
    m_new = jnp.maximum(m, jnp.max(s, axis=-1, keepdims=True))
    p = jnp.exp(s - m_new)
    if valid is not None:
        p = jnp.where(valid, p, 0.0)
    alpha = jnp.exp(m - m_new)
    l = alpha * l + jnp.sum(p, axis=-1, keepdims=True)
    acc = alpha * acc + _dot(p.astype(v.dtype), v)
    return m_new, l, acc


def _fold_lanes(x, op):
    out = x[:, :LANES]
    for u in range(1, x.shape[1] // LANES):
        out = op(out, x[:, u * LANES:(u + 1) * LANES])
    return out


def _row_max_all_lanes(mrun_scr, hh):
    m = jnp.max(mrun_scr[hh], axis=-1, keepdims=True)
    mrun_scr[hh] = jnp.broadcast_to(m, mrun_scr.shape[1:])


def _softmax_value_step(s_scr, mrun_scr, lrun_scr, acc_scr, hh, j, v):
    m = mrun_scr[hh]
    p = jnp.exp(s_scr[hh, j] - jnp.concatenate([m] * (s_scr.shape[3] // LANES), axis=1))
    lrun_scr[hh] = lrun_scr[hh] + _fold_lanes(p, jnp.add)
    acc_scr[hh] = acc_scr[hh] + _dot(p.astype(v.dtype), v)


def _mla_prompt_kernel(qcat_ref, kcat_ref, ga_ref, wuv_ref, out_ref, s_scr, mrun_scr, lrun_scr, acc_scr,
                       *, tq):
    i = pl.program_id(1)
    row = lax.broadcasted_iota(I32, (tq, tq), 0)
    col = lax.broadcasted_iota(I32, (tq, tq), 1)
    causal = col <= row
    mrun_scr[...] = jnp.full(mrun_scr.shape, NEG, F32)

    def score_chunk(j, valid):
        kc = kcat_ref[0, pl.ds(pl.multiple_of(j * tq, tq), tq), :]
        for hh in range(A_HEADS):
            s = _dot_nt(qcat_ref[hh, 0], kc) * MLA_SCALE
            if valid is not None:
                s = jnp.where(valid, s, NEG)
            s_scr[hh, j] = s
            mrun_scr[hh] = jnp.maximum(mrun_scr[hh], _fold_lanes(s, jnp.maximum))

    def far(j, carry):
        score_chunk(j, None)
        return carry

    lax.fori_loop(0, i, far, 0)
    score_chunk(i, causal)
    for hh in range(A_HEADS):
        _row_max_all_lanes(mrun_scr, hh)
    lrun_scr[...] = jnp.zeros(lrun_scr.shape, F32)
    acc_scr[...] = jnp.zeros(acc_scr.shape, F32)

    def value_chunk(j, carry):
        v = kcat_ref[0, pl.ds(pl.multiple_of(j * tq, tq), tq), :KV_LORA]
        for hh in range(A_HEADS):
            _softmax_value_step(s_scr, mrun_scr, lrun_scr, acc_scr, hh, j, v)
        return carry

    lax.fori_loop(0, i + 1, value_chunk, 0)
    for hh in range(A_HEADS):
        l = jnp.sum(lrun_scr[hh], axis=-1, keepdims=True)
        lat = (acc_scr[hh] / l).astype(BF16)
        o = _dot(lat, wuv_ref[hh])
        cols = slice(hh * A_V, (hh + 1) * A_V)
        out_ref[0, :, cols] = (o * _silu(ga_ref[0, :, cols])).astype(BF16)


def _mla_prompt(qcat, kcat, gate_a, w_uv, tq):
    nb, seq = kcat.shape[0], kcat.shape[1]
    return pl.pallas_call(
        functools.partial(_mla_prompt_kernel, tq=tq),
        grid=(nb, seq // tq),
        in_specs=[pl.BlockSpec((A_HEADS, 1, tq, QCAT), lambda b, i: (0, b, i, 0)),
                  pl.BlockSpec((1, seq, QCAT), lambda b, i: (b, 0, 0)),
                  pl.BlockSpec((1, tq, A_WIDTH), lambda b, i: (b, i, 0)),
                  _resident(w_uv.shape)],
        out_specs=pl.BlockSpec((1, tq, A_WIDTH), lambda b, i: (b, i, 0)),
        out_shape=jax.ShapeDtypeStruct((nb, seq, A_WIDTH), BF16),
        scratch_shapes=[pltpu.VMEM((A_HEADS, seq // tq, tq, tq), F32),
                        pltpu.VMEM((A_HEADS, tq, LANES), F32), pltpu.VMEM((A_HEADS, tq, LANES), F32),
                        pltpu.VMEM((A_HEADS, tq, KV_LORA), F32)],
        compiler_params=_params(("parallel", "parallel")),
    )(qcat, kcat, gate_a, w_uv)


def _sort_key(score):
    b = pltpu.bitcast(score + 0.0, I32)
    return jnp.where(b < 0, b ^ 0x7FFFFFFF, b)


def _kth_largest_key(count_ge, shape, k):
    def body(it, lo):
        bit = 31 - it
        cand = lo + lax.shift_left(jnp.int32(1), bit)
        return jnp.where(count_ge(cand) >= k, cand, lo)
    return lax.fori_loop(0, 32, body, jnp.full(shape, INT_MIN, I32), unroll=4)


def _dsa_prompt_kernel(cfar_ref, qi2_ref, wi_ref, ii_ref, qb_ref, kbb_ref, vbb_ref, gb_ref, bias_ref,
                       out_ref, key_scr, thr_scr, s_scr, mrun_scr, lrun_scr, acc_scr, *, tq, seq, k_top, cw):
    i = pl.program_id(1)
    t0 = i * tq
    w = wi_ref[0]
    qrow = t0 + lax.broadcasted_iota(I32, (tq, cw), 0)
    kcol = lax.broadcasted_iota(I32, (tq, cw), 1)

    for c in range(seq // cw):
        @pl.when(c * cw < t0 + tq)
        def _():
            kc = ii_ref[0, c * cw:(c + 1) * cw, :]
            acc = jnp.zeros((tq, cw), F32)
            for hh in range(IDX_HEADS):
                s = _dot_nt(qi2_ref[hh, 0], kc)
                acc = acc + w[:, hh:hh + 1] * jnp.maximum(s, 0.0)
            acc = jnp.where(kcol + c * cw <= qrow, acc, -jnp.inf)
            key = _sort_key(acc)
            for u in range(cw // tq):
                key_scr[c * (cw // tq) + u] = key[:, u * tq:(u + 1) * tq]

    for n in range(1, seq // tq + 1):
        @pl.when(i == n - 1)
        def _():
            def count_ge(cand):
                part = jnp.zeros((tq, LANES), F32)
                for j in range(n):
                    ge = (key_scr[j] >= cand).astype(F32)
                    for u in range(tq // LANES):
                        part = part + ge[:, u * LANES:(u + 1) * LANES]
                return jnp.sum(part, axis=1, keepdims=True)

            thr_scr[...] = _kth_largest_key(count_ge, (tq, 1), k_top)

    thr = thr_scr[...]

    row = lax.broadcasted_iota(I32, (tq, tq), 0)
    col = lax.broadcasted_iota(I32, (tq, tq), 1)
    causal = col <= row
    mrun_scr[...] = jnp.full(mrun_scr.shape, NEG, F32)

    def score_chunk(j, bias_of, extra):
        off = pl.multiple_of(j * tq, tq)
        sel = key_scr[j] >= thr
        if extra is not None:
            sel = sel & extra
        for hh in range(B_HEADS):
            kv_cols = slice((hh // B_GROUP) * B_HD, (hh // B_GROUP + 1) * B_HD)
            qh = qb_ref[0, :, hh * B_HD:(hh + 1) * B_HD]
            s = _dot_nt(qh, kbb_ref[0, pl.ds(off, tq), kv_cols]) * B_SCALE + bias_of(hh)
            s = jnp.where(sel, s, NEG)
            s_scr[hh, j] = s
            mrun_scr[hh] = jnp.maximum(mrun_scr[hh], _fold_lanes(s, jnp.maximum))

    def far(j, carry):
        score_chunk(j, lambda hh: cfar_ref[hh], None)
        return carry

    lax.fori_loop(0, jnp.maximum(i - 1, 0), far, 0)

    @pl.when(i > 0)
    def _():
        score_chunk(i - 1, lambda hh: bias_ref[hh, :, :tq], None)

    score_chunk(i, lambda hh: bias_ref[hh, :, tq:], causal)
    for hh in range(B_HEADS):
        _row_max_all_lanes(mrun_scr, hh)
    lrun_scr[...] = jnp.zeros(lrun_scr.shape, F32)
    acc_scr[...] = jnp.zeros(acc_scr.shape, F32)

    def value_chunk(j, carry):
        off = pl.multiple_of(j * tq, tq)
        for hh in range(B_HEADS):
            kv_cols = slice((hh // B_GROUP) * B_HD, (hh // B_GROUP + 1) * B_HD)
            _softmax_value_step(s_scr, mrun_scr, lrun_scr, acc_scr, hh, j, vbb_ref[0, pl.ds(off, tq), kv_cols])
        return carry

    lax.fori_loop(0, i + 1, value_chunk, 0)
    for hh in range(B_HEADS):
        cols = slice(hh * B_HD, (hh + 1) * B_HD)
        l = jnp.sum(lrun_scr[hh], axis=-1, keepdims=True)
        out_ref[0, :, cols] = ((acc_scr[hh] / l) * _silu(gb_ref[0, :, cols])).astype(BF16)


def _dsa_prompt(cfar, qi2, wi, ii, qb, kbb, vbb, gate_b, bias, tq, k_top):
    nb, seq = ii.shape[0], ii.shape[1]
    cw = min(512, seq)
    assert cw % tq == 0 and seq % cw == 0 and k_top <= tq
    full = lambda w: pl.BlockSpec((1, seq, w), lambda b, i: (b, 0, 0))
    tile = lambda w: pl.BlockSpec((1, tq, w), lambda b, i: (b, i, 0))
    return pl.pallas_call(
        functools.partial(_dsa_prompt_kernel, tq=tq, seq=seq, k_top=k_top, cw=cw),
        grid=(nb, seq // tq),
        in_specs=[pl.BlockSpec(memory_space=pltpu.SMEM),
                  pl.BlockSpec((IDX_HEADS, 1, tq, LANES), lambda b, i: (0, b, i, 0)),
                  tile(LANES), full(LANES), tile(B_WIDTH), full(B_KVW), full(B_KVW), tile(B_WIDTH),
                  _resident(bias.shape)],
        out_specs=tile(B_WIDTH),
        out_shape=jax.ShapeDtypeStruct((nb, seq, B_WIDTH), BF16),
        scratch_shapes=[pltpu.VMEM((seq // tq, tq, tq), I32), pltpu.VMEM((tq, 1), I32),
                        pltpu.VMEM((B_HEADS, seq // tq, tq, tq), F32),
                        pltpu.VMEM((B_HEADS, tq, LANES), F32), pltpu.VMEM((B_HEADS, tq, LANES), F32),
                        pltpu.VMEM((B_HEADS, tq, B_HD), F32)],
        compiler_params=_params(("parallel", "parallel")),
    )(cfar, qi2, wi, ii, qb, kbb, vbb, gate_b, bias)


def _bias_kernel(table_ref, bucket_ref, out_ref):
    bucket = bucket_ref[...]
    for hh in range(B_HEADS):
        acc = jnp.zeros(bucket.shape, F32)
        for b in range(REL_BUCKETS):
            acc = jnp.where(bucket == b, table_ref[b, hh], acc)
        out_ref[hh] = acc


def _t5_bucket_np(rel):
    n = np.maximum(rel, 0)
    exact = REL_BUCKETS // 2
    nf = np.maximum(n, 1).astype(np.float32)
    large = exact + (np.log(nf / exact) / math.log(REL_MAX_DIST / exact)
                     * (REL_BUCKETS - exact)).astype(np.int32)
    return np.where(n < exact, n, np.minimum(large, REL_BUCKETS - 1)).astype(np.int32)


def _bias_tiles(rel_table, rel):
    bucket = jnp.asarray(_t5_bucket_np(rel))
    return pl.pallas_call(
        _bias_kernel,
        in_specs=[pl.BlockSpec(memory_space=pltpu.SMEM), pl.BlockSpec(memory_space=pltpu.VMEM)],
        out_specs=pl.BlockSpec(memory_space=pltpu.VMEM),
        out_shape=jax.ShapeDtypeStruct((B_HEADS,) + rel.shape, F32),
    )(rel_table, bucket)


def _out_proj_kernel(x_ref, ma_ref, mb_ref, p_ref, wout_ref, gple_ref, wgate_ref, wproj_ref, gfin_ref,
                     y_ref, x1_scr, hn_scr, *, cw):
    d = x_ref.shape[1]
    nc = d // cw
    half = ma_ref.shape[1]
    ss = jnp.zeros((x_ref.shape[0], 1), F32)
    for c in range(nc):
        cols = slice(c * cw, (c + 1) * cw)
        x1 = (x_ref[:, cols] + _dot(ma_ref[...], wout_ref[:half, cols])
              + _dot(mb_ref[...], wout_ref[half:, cols]))
        x1_scr[:, cols] = x1
        ss = ss + jnp.sum(x1 * x1, axis=-1, keepdims=True)
    rs = lax.rsqrt(ss / d + EPS)
    for c in range(nc):
        cols = slice(c * cw, (c + 1) * cw)
        hn_scr[:, cols] = (x1_scr[:, cols] * rs * gple_ref[:, cols]).astype(BF16)
    pb = p_ref[...].astype(BF16)
    ss = jnp.zeros((x_ref.shape[0], 1), F32)
    for c in range(nc):
        cols = slice(c * cw, (c + 1) * cw)
        gt = 1.0 / (1.0 + jnp.exp(-_dot(hn_scr[...], wgate_ref[:, cols])))
        x2 = x1_scr[:, cols] + gt * _dot(pb, wproj_ref[:, cols])
        x1_scr[:, cols] = x2
        ss = ss + jnp.sum(x2 * x2, axis=-1, keepdims=True)
    rs = lax.rsqrt(ss / d + EPS)
    for c in range(nc):
        cols = slice(c * cw, (c + 1) * cw)
        y_ref[:, cols] = x1_scr[:, cols] * rs * gfin_ref[:, cols]


def _out_proj(x, ma, mb, p, w_out, g_ple, w_gate, w_proj, g_final, tm):
    n, d = x.shape
    row = lambda w: pl.BlockSpec((tm, w), lambda i: (i, 0))
    return pl.pallas_call(
        functools.partial(_out_proj_kernel, cw=min(512, d)),
        grid=(n // tm,),
        in_specs=[row(d), row(ma.shape[1]), row(mb.shape[1]), row(p.shape[1]),
                  _resident(w_out.shape), _resident(g_ple.shape), _resident(w_gate.shape),
                  _resident(w_proj.shape), _resident(g_final.shape)],
        out_specs=row(d),
        out_shape=jax.ShapeDtypeStruct((n, d), F32),
        scratch_shapes=[pltpu.VMEM((tm, d), F32), pltpu.VMEM((tm, d), BF16)],
        compiler_params=_params(("parallel",)),
    )(x, ma, mb, p, w_out, g_ple, w_gate, w_proj, g_final)


def _page_copies(pt_ref, pools, bufs, lane_major, sems, q, chunk, slot, pages_per_chunk):
    copies = []
    for k, (pool, buf, lm) in enumerate(zip(pools, bufs, lane_major)):
        rows, width = pool.shape[2], pool.shape[3]
        for p in range(pages_per_chunk):
            pid = pt_ref[q, chunk * pages_per_chunk + p]
            if lm:
                dst = buf.at[slot, :, pl.ds(p * width, width)]
            else:
                dst = buf.at[slot, pl.ds(p * rows, rows)]
            copies.append(pltpu.make_async_copy(pool.at[0, pid], dst, sems.at[slot, k]))
    return copies


def _sample_pass1_kernel(pt_ref, qlat_ref, qpe_ref, qidx_ref, wi_ref, ckvn_ref, kpen_ref, kidxn_ref,
                         ga_ref, wuv_ref, pool_ckv, pool_kpe, pool_kidx,
                         out_ref, score_ref,
                         buf_ckv, buf_kpe, buf_kidx, sems, m_scr, l_scr, acc_scr,
                         *, n_chunks, pages_per_chunk, page, past):
    g = pl.program_id(0)
    n_steps = pl.num_programs(0)
    q = g // n_chunks
    c = g % n_chunks
    slot = g % 2
    pools = (pool_ckv, pool_kpe, pool_kidx)
    bufs = (buf_ckv, buf_kpe, buf_kidx)
    ck = pages_per_chunk * page

    def copies(step, slot_):
        return _page_copies(pt_ref, pools, bufs, (False, True, True), sems, step // n_chunks,
                            step % n_chunks, slot_, pages_per_chunk)

    def wait_slot(slot_):
        for k, buf in enumerate(bufs):
            pltpu.make_async_copy(buf.at[slot_], buf.at[slot_], sems.at[slot_, k]).wait()

    @pl.when(g == 0)
    def _():
        for cp in copies(g, slot):
            cp.start()

    @pl.when(g + 1 < n_steps)
    def _():
        for cp in copies(g + 1, 1 - slot):
            cp.start()

    qlat = qlat_ref[0].astype(F32)
    qpe = qpe_ref[0].astype(F32)
    qidx = qidx_ref[0].astype(F32)
    wcol = wi_ref[0]

    @pl.when(c == 0)
    def _():
        ckvn = ckvn_ref[0]
        s_new = (jnp.sum(qlat * ckvn, axis=-1, keepdims=True)
                 + jnp.sum(qpe * kpen_ref[0], axis=-1, keepdims=True)) * MLA_SCALE
        m_scr[...] = s_new
        l_scr[...] = jnp.ones_like(s_new)
        acc_scr[...] = jnp.broadcast_to(ckvn, acc_scr.shape)
        si = jnp.maximum(jnp.sum(qidx * kidxn_ref[0], axis=-1, keepdims=True), 0.0) * wcol
        si = jnp.sum(si, axis=0, keepdims=True)
        lane = lax.broadcasted_iota(I32, (1, SCORE_PAD), 1)
        score_ref[0, :, past:] = jnp.where(lane == 0, si, -jnp.inf)

    wait_slot(slot)

    kc = buf_ckv[slot]
    s = (_dot_nt(qlat, kc) + _dot(qpe, buf_kpe[slot])) * MLA_SCALE
    m, l, acc = _flash_update((m_scr[...], l_scr[...], acc_scr[...]), s, None, kc)
    m_scr[...] = m
    l_scr[...] = l
    acc_scr[...] = acc

    si = jnp.maximum(_dot(qidx, buf_kidx[slot]), 0.0) * wcol
    score_ref[0, :, pl.ds(pl.multiple_of(c * ck, ck), ck)] = jnp.sum(si, axis=0, keepdims=True)

    @pl.when(c == n_chunks - 1)
    def _():
        lat = (acc / l).astype(BF16)
        for hh in range(A_HEADS):
            o = _dot(lat, wuv_ref[hh])[hh:hh + 1]
            cols = slice(hh * A_V, (hh + 1) * A_V)
            out_ref[0, :, cols] = o * _silu(ga_ref[0, :, cols])


def _sample_pass1(page_table, qlat, qpe, qidx, wi, ckvn, kpen, kidxn, gate_a, w_uv,
                  pool_ckv, pool_kpe, pool_kidx, pages_per_chunk):
    nq, n_pages = page_table.shape
    page = pool_ckv.shape[2]
    past = n_pages * page
    n_chunks = n_pages // pages_per_chunk
    ck = pages_per_chunk * page
    per_q = lambda a: pl.BlockSpec((1,) + a.shape[1:], lambda g, pt: (g // n_chunks, 0, 0))
    any_spec = pl.BlockSpec(memory_space=pl.ANY)
    grid_spec = pltpu.PrefetchScalarGridSpec(
        num_scalar_prefetch=1,
        grid=(nq * n_chunks,),
        in_specs=[per_q(qlat), per_q(qpe), per_q(qidx), per_q(wi), per_q(ckvn), per_q(kpen),
                  per_q(kidxn), per_q(gate_a),
                  pl.BlockSpec(w_uv.shape, lambda g, pt: (0, 0, 0), pipeline_mode=pl.Buffered(1)),
                  any_spec, any_spec, any_spec],
        out_specs=[pl.BlockSpec((1, 1, A_WIDTH), lambda g, pt: (g // n_chunks, 0, 0)),
                   pl.BlockSpec((1, 1, past + SCORE_PAD), lambda g, pt: (g // n_chunks, 0, 0))],
        scratch_shapes=[pltpu.VMEM((2, ck, KV_LORA), F32), pltpu.VMEM((2, A_ROPE, ck), F32),
                        pltpu.VMEM((2, IDX_DIM, ck), F32), pltpu.SemaphoreType.DMA((2, 3)),
                        pltpu.VMEM((A_HEADS, 1), F32), pltpu.VMEM((A_HEADS, 1), F32),
                        pltpu.VMEM((A_HEADS, KV_LORA), F32)],
    )
    return pl.pallas_call(
        functools.partial(_sample_pass1_kernel, n_chunks=n_chunks, pages_per_chunk=pages_per_chunk,
                          page=page, past=past),
        grid_spec=grid_spec,
        out_shape=[jax.ShapeDtypeStruct((nq, 1, A_WIDTH), F32),
                   jax.ShapeDtypeStruct((nq, 1, past + SCORE_PAD), F32)],
        compiler_params=_params(("arbitrary",)),
    )(page_table, qlat, qpe, qidx, wi, ckvn, kpen, kidxn, gate_a, w_uv, pool_ckv, pool_kpe, pool_kidx)


def _topk_thresh_kernel(score_ref, thr_ref, *, k_top):
    key = _sort_key(score_ref[...])

    def count_ge(cand):
        ge = (key >= cand).astype(F32)
        return jnp.sum(jnp.sum(ge, axis=1, keepdims=True), axis=2, keepdims=True)

    thr = _kth_largest_key(count_ge, (key.shape[0], 1, 1), k_top)
    thr_ref[...] = jnp.broadcast_to(thr, thr_ref.shape)


def _topk_thresh(score3, k_top):
    nq = score3.shape[0]
    return pl.pallas_call(
        functools.partial(_topk_thresh_kernel, k_top=k_top),
        in_specs=[pl.BlockSpec(memory_space=pltpu.VMEM)],
        out_specs=pl.BlockSpec(memory_space=pltpu.VMEM),
        out_shape=jax.ShapeDtypeStruct((nq, 1, LANES), I32),
        compiler_params=_params(),
    )(score3)


ROW_TILE = 8
SLOTS_PER_TILE = ROW_TILE // B_KV_HEADS


def _select_rows_kernel(score_ref, thr_ref, pt_ref, tix_ref, sub_ref, nsel_ref, *, n_pages, page, n_rows):
    key = _sort_key(score_ref[0, :n_pages, :])
    prow = lax.broadcasted_iota(I32, (n_pages, page), 0)
    m = ((key >= thr_ref[0][:, :1]) & (prow < n_pages - 1)).astype(F32)
    r_i = lax.broadcasted_iota(I32, (n_pages, n_pages), 0)
    c_i = lax.broadcasted_iota(I32, (n_pages, n_pages), 1)
    cnt = _dot(m.astype(BF16), jnp.ones((page, LANES), BF16))
    o_excl = _dot((c_i < r_i).astype(BF16), cnt.astype(BF16))
    o_incl = o_excl + cnt
    reps = n_rows // LANES
    o_ex = jnp.concatenate([o_excl] * reps, axis=1)
    o_in = jnp.concatenate([o_incl] * reps, axis=1)
    jrow = lax.broadcasted_iota(I32, (n_pages, n_rows), 1).astype(F32)
    hit = ((o_ex <= jrow) & (jrow < o_in)).astype(F32)
    pt_j = jnp.sum(hit * pt_ref[0], axis=0, keepdims=True)
    o_j = jnp.sum(hit * o_ex, axis=0, keepdims=True)
    used = jnp.sum(hit, axis=0, keepdims=True)
    rank = jrow[:1] - o_j
    s_r = lax.broadcasted_iota(I32, (page, page), 0)
    s_c = lax.broadcasted_iota(I32, (page, page), 1)
    pst = _dot((s_c <= s_r).astype(BF16), m.T.astype(BF16))
    psj = _dot(pst.astype(BF16), hit.astype(BF16))
    slot_j = jnp.sum((psj <= rank).astype(F32), axis=0, keepdims=True)
    quad = jnp.floor(slot_j * (1.0 / SLOTS_PER_TILE))
    tile = pt_j * (page // SLOTS_PER_TILE) + quad
    tix_ref[0] = jnp.where(used > 0.0, tile, 0.0).astype(I32)
    sub_ref[0] = ((slot_j - SLOTS_PER_TILE * quad) * B_KV_HEADS).astype(I32)
    nsel_ref[0] = o_incl[n_pages - 1:n_pages, :].astype(I32)


def _select_rows(score3, thr, pt_col, n_rows):
    nq, rows, page = score3.shape
    n_pages = pt_col.shape[1]
    assert n_rows % LANES == 0 and page == LANES and n_pages % 8 == 0
    per_q = lambda a: pl.BlockSpec((1,) + a.shape[1:], lambda q: (q, 0, 0))
    out = lambda w: pl.BlockSpec((1, 1, w), lambda q: (q, 0, 0))
    return pl.pallas_call(
        functools.partial(_select_rows_kernel, n_pages=n_pages, page=page, n_rows=n_rows),
        grid=(nq,),
        in_specs=[per_q(score3), per_q(thr), per_q(pt_col)],
        out_specs=[out(n_rows), out(n_rows), out(LANES)],
        out_shape=[jax.ShapeDtypeStruct((nq, 1, n_rows), I32), jax.ShapeDtypeStruct((nq, 1, n_rows), I32),
                   jax.ShapeDtypeStruct((nq, 1, LANES), I32)],
        compiler_params=_params(("parallel",)),
    )(score3, thr, pt_col)


def _sample_pass2_kernel(pt_ref, tix_ref, cfar_ref, qb_ref, kbn_ref, vbn_ref, gb_ref, sub_ref, nsel_ref,
                         slast_ref, snew_ref, thr_ref, tail_ref, bnew_ref, pool_k, pool_v, out_ref,
                         buf_k, buf_v, sems, *, n_pages, page, n_rows):
    q = pl.program_id(0)
    nq = pl.num_programs(0)
    slot = q % 2
    gw = ROW_TILE * n_rows
    lw = B_KV_HEADS * page
    pools_bufs = ((pool_k, buf_k), (pool_v, buf_v))

    def start_all(qq, slot_):
        for j in range(n_rows):
            off = pl.multiple_of(tix_ref[qq, j] * ROW_TILE, ROW_TILE)
            for k, (pool, buf) in enumerate(pools_bufs):
                pltpu.make_async_copy(pool.at[pl.ds(off, ROW_TILE)],
                                      buf.at[slot_, pl.ds(j * ROW_TILE, ROW_TILE)], sems.at[slot_, k]).start()
        last = pl.multiple_of(pt_ref[qq, n_pages - 1] * lw, lw)
        for k, (pool, buf) in enumerate(pools_bufs):
            pltpu.make_async_copy(pool.at[pl.ds(last, lw)], buf.at[slot_, pl.ds(gw, lw)],
                                  sems.at[slot_, k]).start()

    @pl.when(q == 0)
    def _():
        start_all(q, slot)

    @pl.when(q + 1 < nq)
    def _():
        start_all(q + 1, 1 - slot)

    qb = qb_ref[0].astype(F32)
    thr = thr_ref[0][:, :1]
    kv_of_row = (lax.broadcasted_iota(I32, (B_HEADS, 1), 0) >= B_GROUP).astype(I32)
    head0 = kv_of_row == 0
    lo, hi = slice(0, B_HD), slice(B_HD, 2 * B_HD)

    keep = _sort_key(snew_ref[0, :, :LANES])[:, :1] >= thr
    kbn = kbn_ref[0]
    vbn = vbn_ref[0]
    s_new = jnp.where(head0, jnp.sum(qb * kbn[:, lo], axis=-1, keepdims=True),
                      jnp.sum(qb * kbn[:, hi], axis=-1, keepdims=True))
    s_new = jnp.where(keep, s_new * B_SCALE + bnew_ref[...], NEG)
    v_new = jnp.where(head0, jnp.broadcast_to(vbn[:, lo], (B_HEADS, B_HD)),
                      jnp.broadcast_to(vbn[:, hi], (B_HEADS, B_HD)))

    lane_g = lax.broadcasted_iota(I32, (B_HEADS, gw), 1)
    valid_g = ((lane_g & (ROW_TILE - 1)) == sub_ref[0] + kv_of_row) & (lane_g < ROW_TILE * nsel_ref[0][:, :1])
    lane_l = lax.broadcasted_iota(I32, (B_HEADS, lw), 1)
    sel_l = _sort_key(slast_ref[0]) >= thr
    valid_l = jnp.broadcast_to(sel_l, (B_HEADS, lw)) & ((lane_l & 1) == kv_of_row)
    valid = jnp.concatenate([valid_g, valid_l], axis=1)
    bias = jnp.concatenate([jnp.broadcast_to(cfar_ref[...], (B_HEADS, gw)), tail_ref[...]], axis=1)

    for k, (pool, buf) in enumerate(pools_bufs):
        pltpu.make_async_copy(pool.at[pl.ds(0, gw + lw)], buf.at[slot], sems.at[slot, k]).wait()

    s = _dot_nt(qb, buf_k[slot]) * B_SCALE + bias
    s = jnp.where(valid, s, NEG)
    m = jnp.maximum(jnp.max(s, axis=-1, keepdims=True), s_new)
    p = jnp.where(valid, jnp.exp(s - m), 0.0)
    p_new = jnp.where(keep, jnp.exp(s_new - m), 0.0)
    l = jnp.sum(p, axis=-1, keepdims=True) + p_new
    o = (_dot(p, buf_v[slot]) + p_new * v_new) / l
    for hh in range(B_HEADS):
        cols = slice(hh * B_HD, (hh + 1) * B_HD)
        out_ref[0, :, cols] = o[hh:hh + 1] * _silu(gb_ref[0, :, cols])


def _sample_pass2(page_table, tix, cfar, qb, kbn, vbn, gate_b, sub_rep, nsel, slast2, score, thr, tail2,
                  bias_new, pool_k2, pool_v2, page):
    nq, n_pages = page_table.shape
    n_rows = tix.shape[1]
    past = n_pages * page
    rows = ROW_TILE * n_rows + B_KV_HEADS * page
    assert past % SCORE_PAD == 0
    per_q = lambda a: pl.BlockSpec((1,) + a.shape[1:], lambda q, pt, tx: (q, 0, 0))
    const2 = lambda a: pl.BlockSpec(a.shape, lambda q, pt, tx: (0, 0))
    any_spec = pl.BlockSpec(memory_space=pl.ANY)
    grid_spec = pltpu.PrefetchScalarGridSpec(
        num_scalar_prefetch=2,
        grid=(nq,),
        in_specs=[const2(cfar), per_q(qb), per_q(kbn), per_q(vbn), per_q(gate_b), per_q(sub_rep),
                  per_q(nsel), per_q(slast2),
                  pl.BlockSpec((1, 1, SCORE_PAD), lambda q, pt, tx: (q, 0, past // SCORE_PAD)),
                  per_q(thr), const2(tail2), const2(bias_new), any_spec, any_spec],
        out_specs=pl.BlockSpec((1, 1, B_WIDTH), lambda q, pt, tx: (q, 0, 0)),
        scratch_shapes=[pltpu.VMEM((2, rows, B_HD), F32), pltpu.VMEM((2, rows, B_HD), F32),
                        pltpu.SemaphoreType.DMA((2, 2))],
    )
    return pl.pallas_call(
        functools.partial(_sample_pass2_kernel, n_pages=n_pages, page=page, n_rows=n_rows),
        grid_spec=grid_spec,
        out_shape=jax.ShapeDtypeStruct((nq, 1, B_WIDTH), F32),
        compiler_params=_params(("arbitrary",)),
    )(page_table, tix, cfar, qb, kbn, vbn, gate_b, sub_rep, nsel, slast2, score, thr, tail2, bias_new,
      pool_k2, pool_v2)


def _prep_w_in(w_in):
    offs = np.concatenate([[0], np.cumsum(IN_SIZES)])
    part = lambda k: w_in[:, offs[k]:offs[k + 1]]
    c_q, c_kv, k_pe, gate_a, q_b, k_b, v_b, gate_b, q_idx, k_idx, w_idx = [part(k) for k in range(11)]
    pad = jnp.zeros((w_in.shape[0], LANES - IDX_HEADS), w_in.dtype)
    cols = [c_q, c_kv, gate_a, q_b, k_b, v_b, gate_b, q_idx, k_pe, k_idx, w_idx, pad]
    return jnp.concatenate(cols, axis=1).astype(BF16)


def _prep_w_uq(w_uq):
    w = w_uq.reshape(w_uq.shape[0], A_HEADS, A_NOPE + A_ROPE)
    nope = w[:, :, :A_NOPE].reshape(w_uq.shape[0], A_HEADS * A_NOPE)
    rope = w[:, :, A_NOPE:].reshape(w_uq.shape[0], A_HEADS * A_ROPE)
    return jnp.concatenate([nope, rope], axis=1).astype(BF16)


def kernel(x_prompt, x_sample, p_prompt, p_sample, cache_ckv, cache_kpe, cache_k, cache_v, cache_kidx,
           page_table, rel_table, g_attn, w_in, g_q, w_uq, g_kv, w_uk, w_uv, w_out, g_ple, w_ple_gate,
           w_ple_proj, g_final):
    nb, seq, d = x_prompt.shape
    nq, nt, _ = x_sample.shape
    depth = w_in.shape[0]
    assert depth == 1 and nt == 1
    n_pages = page_table.shape[1]
    page = cache_ckv.shape[2]
    past = n_pages * page
    tq = min(256, seq)
    tm_s = min(128, nq)
    pages_per_chunk = min(32, n_pages)
    assert seq % tq == 0 and nq % tm_s == 0 and n_pages % pages_per_chunk == 0
    assert tq + 1 > REL_MAX_DIST

    row2 = lambda v: v.reshape(1, -1)
    w_in_p = _prep_w_in(w_in[0])
    w_uq_p = _prep_w_uq(w_uq[0])
    w_uk_b = w_uk[0].astype(BF16)
    w_uv_b = w_uv[0].astype(BF16)
    w_out_b = w_out[0].astype(BF16)
    w_gate_b = w_ple_gate[0].astype(BF16)
    w_proj_b = w_ple_proj[0].astype(BF16)
    cfar = rel_table[REL_BUCKETS - 1]

    tabs_p = _rope_tables(jnp.arange(seq, dtype=I32))
    xp = x_prompt.reshape(nb * seq, d)
    (qcat, qi2, _, _, wi, qb, ga, gb, ckv, kpe, kb, vb, kidx, kcat, ii, kbb, vbb) = _proj_in(
        xp, tabs_p, seq // tq, row2(g_attn[0]), w_in_p, row2(g_q[0]), w_uq_p, row2(g_kv[0]), w_uk_b, tq)
    b3 = lambda a: a.reshape((nb, seq) + a.shape[1:])
    b4 = lambda a: a.reshape((a.shape[0], nb, seq) + a.shape[2:])
    ma = _mla_prompt(b4(qcat), b3(kcat), b3(ga), w_uv_b, tq)
    rel = np.arange(tq)[:, None] + tq - np.arange(2 * tq)[None, :]
    bias = _bias_tiles(rel_table, rel)
    k_top = min(IDX_TOPK, seq // 4)
    mb = _dsa_prompt(cfar, b4(qi2), b3(wi), b3(ii), b3(qb), b3(kbb), b3(vbb), b3(gb), bias, tq, k_top)
    y_prompt = _out_proj(xp, ma.reshape(nb * seq, -1), mb.reshape(nb * seq, -1),
                         p_prompt[0].reshape(nb * seq, -1), w_out_b, row2(g_ple[0]), w_gate_b, w_proj_b,
                         row2(g_final), tq).reshape(nb, seq, d)

    tabs_s = _rope_tables(jnp.full((tm_s,), past, I32))
    xs = x_sample.reshape(nq, d)
    (qcat_s, _, qpe_s, qidx_s, wi_s, qb_s, ga_s, gb_s, ckv_s, kpe_s, kb_s, vb_s, kidx_s, _, _, _, _) = _proj_in(
        xs, tabs_s, 1, row2(g_attn[0]), w_in_p, row2(g_q[0]), w_uq_p, row2(g_kv[0]), w_uk_b, tm_s)
    qlat_s = jnp.swapaxes(qcat_s[:, :, :KV_LORA], 0, 1)
    ma_s, score = _sample_pass1(
        page_table, qlat_s, qpe_s.reshape(nq, A_HEADS, A_ROPE), qidx_s.reshape(nq, IDX_HEADS, IDX_DIM),
        wi_s[:, :IDX_HEADS].reshape(nq, IDX_HEADS, 1), ckv_s.reshape(nq, 1, -1), kpe_s.reshape(nq, 1, -1),
        kidx_s.reshape(nq, 1, -1), ga_s.reshape(nq, 1, -1), w_uv_b, cache_ckv,
        jnp.swapaxes(cache_kpe, 2, 3), jnp.swapaxes(cache_kidx, 2, 3), pages_per_chunk)
    k_top_s = min(IDX_TOPK, (past + nt) // 4)
    score3 = score.reshape(nq, n_pages + SCORE_PAD // LANES, LANES)
    thr = _topk_thresh(score3, k_top_s)
    tix, sub, nsel = _select_rows(score3, thr, page_table.astype(F32).reshape(nq, n_pages, 1), k_top_s)
    rel_tail = np.broadcast_to(np.concatenate(
        [np.repeat(page - np.arange(page), 2), np.zeros(LANES, np.int64)])[None, :], (8, 2 * page + LANES))
    tail = _bias_tiles(rel_table, rel_tail)[:, 0, :]
    rows2 = lambda pool: pool.reshape(pool.shape[1] * B_KV_HEADS * page, B_HD)
    mb_s = _sample_pass2(page_table, tix.reshape(nq, -1), cfar.reshape(B_HEADS, 1),
                         qb_s.reshape(nq, B_HEADS, B_HD), kb_s.reshape(nq, 1, -1), vb_s.reshape(nq, 1, -1),
                         gb_s.reshape(nq, 1, -1), jnp.repeat(sub, ROW_TILE, axis=2), nsel,
                         jnp.repeat(score[:, :, past - page:past], 2, axis=2), score, thr,
                         tail[:, :2 * page], tail[:, 2 * page:2 * page + 1],
                         rows2(cache_k), rows2(cache_v), page)
    y_sample = _out_proj(xs, ma_s.reshape(nq, -1).astype(BF16), mb_s.reshape(nq, -1).astype(BF16),
                         p_sample[0].reshape(nq, -1), w_out_b, row2(g_ple[0]), w_gate_b, w_proj_b,
                         row2(g_final), tm_s).reshape(nq, nt, d)

    return (y_prompt, y_sample,
            ckv.reshape(1, nb, seq, -1), kpe.reshape(1, nb, seq, -1),
            kb.reshape(1, nb, seq, B_KV_HEADS, B_HD), vb.reshape(1, nb, seq, B_KV_HEADS, B_HD),
            kidx.reshape(1, nb, seq, -1),
            ckv_s.reshape(1, nq, nt, -1), kpe_s.reshape(1, nq, nt, -1),
            kb_s.reshape(1, nq, nt, B_KV_HEADS, B_HD), vb_s.reshape(1, nq, nt, B_KV_HEADS, B_HD),
            kidx_s.reshape(1, nq, nt, -1))
```

```python
import functools
import math

import numpy as np
import jax
import jax.numpy as jnp
from jax import lax
from jax.experimental import pallas as pl
from jax.experimental.pallas import tpu as pltpu

F32 = jnp.float32
BF16 = jnp.bfloat16
I32 = jnp.int32

A_HEADS = 8
A_NOPE = 128
A_ROPE = 64
A_V = 128
A_WIDTH = A_HEADS * A_V
Q_LORA = 512
KV_LORA = 256
B_HEADS = 8
B_KV_HEADS = 2
B_HD = 128
B_GROUP = B_HEADS // B_KV_HEADS
B_WIDTH = B_HEADS * B_HD
B_KVW = B_KV_HEADS * B_HD
IDX_HEADS = 16
IDX_DIM = 64
IDX_ROPE = 32
IDX_TOPK = 256
REL_BUCKETS = 32
REL_MAX_DIST = 128
ROPE_THETA = 10000.0
EPS = 1e-6
MLA_SCALE = (A_NOPE + A_ROPE) ** -0.5
B_SCALE = B_HD ** -0.5
IDX_W_SCALE = (IDX_HEADS ** -0.5) * (IDX_DIM ** -0.5)
IN_SIZES = (Q_LORA, KV_LORA, A_ROPE, A_WIDTH, B_WIDTH, B_KVW, B_KVW, B_WIDTH,
            IDX_HEADS * IDX_DIM, IDX_DIM, IDX_HEADS)

LANES = 128
NEG = -1e30
VMEM_LIMIT = 56 * 1024 * 1024
INT_MIN = -2 ** 31

C_Q = 0
C_KV = C_Q + Q_LORA
C_GA = C_KV + KV_LORA
C_QB = C_GA + A_WIDTH
C_KB = C_QB + B_WIDTH
C_VB = C_KB + B_KVW
C_GB = C_VB + B_KVW
C_QI = C_GB + B_WIDTH
C_SLAB = C_QI + IDX_HEADS * IDX_DIM
C_WI = C_SLAB + LANES
NP_IN = C_WI + LANES
QCAT = KV_LORA + LANES
N_TAB = 11
SCORE_PAD = 8 * LANES


def _dot(a, b):
    return jnp.dot(a, b, preferred_element_type=F32)


def _dot_nt(a, b):
    return lax.dot_general(a, b, (((1,), (1,)), ((), ())), preferred_element_type=F32)


def _silu(x):
    return x / (1.0 + jnp.exp(-x))


def _params(sem=None):
    return pltpu.CompilerParams(dimension_semantics=sem, vmem_limit_bytes=VMEM_LIMIT)


def _resident(shape):
    nd = len(shape)
    return pl.BlockSpec(shape, lambda *_: (0,) * nd, pipeline_mode=pl.Buffered(1))


def _rope_tables(pos):
    def cs(half):
        inv = ROPE_THETA ** (-jnp.arange(half, dtype=F32) / half)
        ang = pos.astype(F32)[:, None] * inv[None, :]
        return jnp.cos(ang), jnp.sin(ang)
    n = pos.shape[0]
    c32, s32 = cs(A_ROPE // 2)
    c16, s16 = cs(IDX_ROPE // 2)
    z16 = jnp.zeros((n, 16), F32)
    z32 = jnp.zeros((n, 32), F32)
    z64 = jnp.zeros((n, 64), F32)
    o32 = jnp.ones((n, 32), F32)
    cat = lambda *a: jnp.concatenate(a, axis=1)
    a64, bm64, bp64 = cat(c32, c32), cat(-s32, z32), cat(z32, s32)
    a16, cm16, cp16 = cat(c16, c16, o32), cat(-s16, z16, z32), cat(z16, s16, z32)
    tabs = [cat(a64, a64), cat(bm64, bm64), cat(bp64, bp64),
            cat(a16, a16), cat(cm16, cm16), cat(cp16, cp16),
            cat(a64, a16), cat(bm64, z64), cat(bp64, z64), cat(z64, cm16), cat(z64, cp16)]
    return jnp.concatenate(tabs, axis=1)


def _proj_in_kernel(x_ref, gattn_ref, win_ref, gq_ref, wuq_ref, gkv_ref, wuk_ref, tab_ref,
                    qcat_ref, qi2_ref, qpe_ref, qidx_ref, wi_ref, qb_ref, ga_ref, gb_ref,
                    ckv_ref, kpe_ref, kb_ref, vb_ref, kidx_ref, kcat_ref, ii_ref, kbb_ref, vbb_ref,
                    h_scr):
    x = x_ref[...]
    ms = jnp.mean(x * x, axis=-1, keepdims=True)
    h_scr[...] = (x * lax.rsqrt(ms + EPS) * gattn_ref[...]).astype(BF16)

    def zcols(c0, width):
        return _dot(h_scr[...], win_ref[:, c0:c0 + width])

    def tab(k):
        return tab_ref[:, k * LANES:(k + 1) * LANES]

    tm = x.shape[0]
    lane = lax.broadcasted_iota(I32, (tm, LANES), 1)
    lo = lane < 64

    def rope(xg, a, bm, bp, shift):
        return (xg * tab(a) + pltpu.roll(xg, LANES - shift, 1) * tab(bm)
                + pltpu.roll(xg, shift, 1) * tab(bp))

    cq = zcols(C_Q, Q_LORA)
    cq = cq * lax.rsqrt(jnp.mean(cq * cq, axis=-1, keepdims=True) + EPS) * gq_ref[...]
    q = _dot(cq.astype(BF16), wuq_ref[...])
    n_nope = A_HEADS * A_NOPE
    for g in range(A_HEADS // 2):
        xg = q[:, n_nope + g * LANES:n_nope + (g + 1) * LANES]
        rg = rope(xg, 0, 1, 2, A_ROPE // 2)
        qpe_ref[:, g * LANES:(g + 1) * LANES] = rg.astype(BF16)
        qcat_ref[2 * g, :, KV_LORA:] = jnp.where(lo, rg, 0.0).astype(BF16)
        qcat_ref[2 * g + 1, :, KV_LORA:] = jnp.where(lo, 0.0, rg).astype(BF16)
    for hh in range(A_HEADS):
        qn = q[:, hh * A_NOPE:(hh + 1) * A_NOPE].astype(BF16)
        qcat_ref[hh, :, :KV_LORA] = _dot(qn, wuk_ref[hh]).astype(BF16)

    ckv = zcols(C_KV, KV_LORA)
    ckv = ckv * lax.rsqrt(jnp.mean(ckv * ckv, axis=-1, keepdims=True) + EPS) * gkv_ref[...]
    ckv_ref[...] = ckv
    kcat_ref[:, :KV_LORA] = ckv.astype(BF16)

    slab = zcols(C_SLAB, LANES)
    slab = (slab * tab(6) + pltpu.roll(slab, LANES - 32, 1) * tab(7) + pltpu.roll(slab, 32, 1) * tab(8)
            + pltpu.roll(slab, LANES - 16, 1) * tab(9) + pltpu.roll(slab, 16, 1) * tab(10))
    swapped = pltpu.roll(slab, 64, 1)
    kpe_ref[...] = slab[:, :A_ROPE]
    kidx_ref[...] = swapped[:, :IDX_DIM]
    kcat_ref[:, KV_LORA:] = jnp.where(lo, slab, swapped).astype(BF16)
    ii_ref[...] = jnp.where(lo, swapped, slab).astype(BF16)

    kb = zcols(C_KB, B_KVW)
    kb_ref[...] = kb
    kbb_ref[...] = kb.astype(BF16)
    vb = zcols(C_VB, B_KVW)
    vb_ref[...] = vb
    vbb_ref[...] = vb.astype(BF16)
    for c in range(B_WIDTH // 512):
        qb_ref[:, c * 512:(c + 1) * 512] = zcols(C_QB + c * 512, 512).astype(BF16)
        ga_ref[:, c * 512:(c + 1) * 512] = zcols(C_GA + c * 512, 512)
        gb_ref[:, c * 512:(c + 1) * 512] = zcols(C_GB + c * 512, 512)

    for c in range(IDX_HEADS * IDX_DIM // 512):
        qi = zcols(C_QI + c * 512, 512)
        for g in range(4):
            rg = rope(qi[:, g * LANES:(g + 1) * LANES], 3, 4, 5, IDX_ROPE // 2)
            col = c * 512 + g * LANES
            qidx_ref[:, col:col + LANES] = rg.astype(BF16)
            hh = col // IDX_DIM
            qi2_ref[hh] = jnp.where(lo, rg, 0.0).astype(BF16)
            qi2_ref[hh + 1] = jnp.where(lo, 0.0, rg).astype(BF16)
    wi_ref[...] = zcols(C_WI, LANES) * IDX_W_SCALE


def _proj_in(x, tabs, tab_blocks, g_attn, w_in_p, g_q, w_uq_p, g_kv, w_uk, tm):
    n, d = x.shape
    grid = (n // tm,)
    row = lambda w: pl.BlockSpec((tm, w), lambda i: (i, 0))
    head = lambda nh, w: pl.BlockSpec((nh, tm, w), lambda i: (0, i, 0))
    out_shape = [
        jax.ShapeDtypeStruct((A_HEADS, n, QCAT), BF16),
        jax.ShapeDtypeStruct((IDX_HEADS, n, LANES), BF16),
        jax.ShapeDtypeStruct((n, A_HEADS * A_ROPE), BF16),
        jax.ShapeDtypeStruct((n, IDX_HEADS * IDX_DIM), BF16),
        jax.ShapeDtypeStruct((n, LANES), F32),
        jax.ShapeDtypeStruct((n, B_WIDTH), BF16),
        jax.ShapeDtypeStruct((n, A_WIDTH), F32),
        jax.ShapeDtypeStruct((n, B_WIDTH), F32),
        jax.ShapeDtypeStruct((n, KV_LORA), F32),
        jax.ShapeDtypeStruct((n, A_ROPE), F32),
        jax.ShapeDtypeStruct((n, B_KVW), F32),
        jax.ShapeDtypeStruct((n, B_KVW), F32),
        jax.ShapeDtypeStruct((n, IDX_DIM), F32),
        jax.ShapeDtypeStruct((n, QCAT), BF16),
        jax.ShapeDtypeStruct((n, LANES), BF16),
        jax.ShapeDtypeStruct((n, B_KVW), BF16),
        jax.ShapeDtypeStruct((n, B_KVW), BF16),
    ]
    out_specs = [head(A_HEADS, QCAT), head(IDX_HEADS, LANES), row(A_HEADS * A_ROPE),
                 row(IDX_HEADS * IDX_DIM), row(LANES), row(B_WIDTH), row(A_WIDTH), row(B_WIDTH),
                 row(KV_LORA), row(A_ROPE), row(B_KVW), row(B_KVW), row(IDX_DIM),
                 row(QCAT), row(LANES), row(B_KVW), row(B_KVW)]
    in_specs = [row(d), _resident(g_attn.shape), _resident(w_in_p.shape), _resident(g_q.shape),
                _resident(w_uq_p.shape), _resident(g_kv.shape), _resident(w_uk.shape),
                pl.BlockSpec((tm, N_TAB * LANES), lambda i: (i % tab_blocks, 0))]
    return pl.pallas_call(
        _proj_in_kernel, grid=grid, in_specs=in_specs, out_specs=out_specs, out_shape=out_shape,
        scratch_shapes=[pltpu.VMEM((tm, d), BF16)],
        compiler_params=_params(("parallel",)),
    )(x, g_attn, w_in_p, g_q, w_uq_p, g_kv, w_uk, tabs)


def _flash_update(carry, s, valid, v):
    m, l, acc = carry
    if valid is not None:
        s = jnp.where(valid, s, NEG)
    m_new = jnp.maximum(m, jnp.max(s, axis=-1, keepdims=True))
    p = jnp.exp(s - m_new)
    if valid is not None:
        p = jnp.where(valid, p, 0.0)
    alpha = jnp.exp(m - m_new)
    l = alpha * l + jnp.sum(p, axis=-1, keepdims=True)
    acc = alpha * acc + _dot(p.astype(v.dtype), v)
    return m_new, l, acc


def _fold_lanes(x, op):
    out = x[:, :LANES]
    for u in range(1, x.shape[1] // LANES):
        out = op(out, x[:, u * LANES:(u + 1) * LANES])
    return out


def _row_max_all_lanes(mrun_scr, hh):
    m = jnp.max(mrun_scr[hh], axis=-1, keepdims=True)
    mrun_scr[hh] = jnp.broadcast_to(m, mrun_scr.shape[1:])


def _softmax_value_step(s_scr, mrun_scr, lrun_scr, acc_scr, hh, j, v):
    m = mrun_scr[hh]
    p = jnp.exp(s_scr[hh, j] - jnp.concatenate([m] * (s_scr.shape[3] // LANES), axis=1))
    lrun_scr[hh] = lrun_scr[hh] + _fold_lanes(p, jnp.add)
    acc_scr[hh] = acc_scr[hh] + _dot(p.astype(v.dtype), v)


def _mla_prompt_kernel(qcat_ref, kcat_ref, ga_ref, wuv_ref, out_ref, s_scr, mrun_scr, lrun_scr, acc_scr,
                       *, tq):
    i = pl.program_id(1)
    row = lax.broadcasted_iota(I32, (tq, tq), 0)
    col = lax.broadcasted_iota(I32, (tq, tq), 1)
    causal = col <= row
    mrun_scr[...] = jnp.full(mrun_scr.shape, NEG, F32)

    def score_chunk(j, valid):
        kc = kcat_ref[0, pl.ds(pl.multiple_of(j * tq, tq), tq), :]
        for hh in range(A_HEADS):
            s = _dot_nt(qcat_ref[hh, 0], kc) * MLA_SCALE
            if valid is not None:
                s = jnp.where(valid, s, NEG)
            s_scr[hh, j] = s
            mrun_scr[hh] = jnp.maximum(mrun_scr[hh], _fold_lanes(s, jnp.maximum))

    def far(j, carry):
        score_chunk(j, None)
        return carry

    lax.fori_loop(0, i, far, 0)
    score_chunk(i, causal)
    for hh in range(A_HEADS):
        _row_max_all_lanes(mrun_scr, hh)
    lrun_scr[...] = jnp.zeros(lrun_scr.shape, F32)
    acc_scr[...] = jnp.zeros(acc_scr.shape, F32)

    def value_chunk(j, carry):
        v = kcat_ref[0, pl.ds(pl.multiple_of(j * tq, tq), tq), :KV_LORA]
        for hh in range(A_HEADS):
            _softmax_value_step(s_scr, mrun_scr, lrun_scr, acc_scr, hh, j, v)
        return carry

    lax.fori_loop(0, i + 1, value_chunk, 0)
    for hh in range(A_HEADS):
        l = jnp.sum(lrun_scr[hh], axis=-1, keepdims=True)
        lat = (acc_scr[hh] / l).astype(BF16)
        o = _dot(lat, wuv_ref[hh])
        cols = slice(hh * A_V, (hh + 1) * A_V)
        out_ref[0, :, cols] = (o * _silu(ga_ref[0, :, cols])).astype(BF16)


def _mla_prompt(qcat, kcat, gate_a, w_uv, tq):
    nb, seq = kcat.shape[0], kcat.shape[1]
    return pl.pallas_call(
        functools.partial(_mla_prompt_kernel, tq=tq),
        grid=(nb, seq // tq),
        in_specs=[pl.BlockSpec((A_HEADS, 1, tq, QCAT), lambda b, i: (0, b, i, 0)),
                  pl.BlockSpec((1, seq, QCAT), lambda b, i: (b, 0, 0)),
                  pl.BlockSpec((1, tq, A_WIDTH), lambda b, i: (b, i, 0)),
                  _resident(w_uv.shape)],
        out_specs=pl.BlockSpec((1, tq, A_WIDTH), lambda b, i: (b, i, 0)),
        out_shape=jax.ShapeDtypeStruct((nb, seq, A_WIDTH), BF16),
        scratch_shapes=[pltpu.VMEM((A_HEADS, seq // tq, tq, tq), F32),
                        pltpu.VMEM((A_HEADS, tq, LANES), F32), pltpu.VMEM((A_HEADS, tq, LANES), F32),
                        pltpu.VMEM((A_HEADS, tq, KV_LORA), F32)],
        compiler_params=_params(("parallel", "parallel")),
    )(qcat, kcat, gate_a, w_uv)


def _sort_key(score):
    b = pltpu.bitcast(score + 0.0, I32)
    return jnp.where(b < 0, b ^ 0x7FFFFFFF, b)


def _kth_largest_key(count_ge, shape, k):
    def body(it, lo):
        bit = 31 - it
        cand = lo + lax.shift_left(jnp.int32(1), bit)
        return jnp.where(count_ge(cand) >= k, cand, lo)
    return lax.fori_loop(0, 32, body, jnp.full(shape, INT_MIN, I32), unroll=4)


def _dsa_prompt_kernel(cfar_ref, qi2_ref, wi_ref, ii_ref, qb_ref, kbb_ref, vbb_ref, gb_ref, bias_ref,
                       out_ref, key_scr, thr_scr, s_scr, mrun_scr, lrun_scr, acc_scr, *, tq, seq, k_top, cw):
    i = pl.program_id(1)
    t0 = i * tq
    w = wi_ref[0]
    qrow = t0 + lax.broadcasted_iota(I32, (tq, cw), 0)
    kcol = lax.broadcasted_iota(I32, (tq, cw), 1)

    for c in range(seq // cw):
        @pl.when(c * cw < t0 + tq)
        def _():
            kc = ii_ref[0, c * cw:(c + 1) * cw, :]
            acc = jnp.zeros((tq, cw), F32)
            for hh in range(IDX_HEADS):
                s = _dot_nt(qi2_ref[hh, 0], kc)
                acc = acc + w[:, hh:hh + 1] * jnp.maximum(s, 0.0)
            acc = jnp.where(kcol + c * cw <= qrow, acc, -jnp.inf)
            key = _sort_key(acc)
            for u in range(cw // tq):
                key_scr[c * (cw // tq) + u] = key[:, u * tq:(u + 1) * tq]

    for n in range(1, seq // tq + 1):
        @pl.when(i == n - 1)
        def _():
            def count_ge(cand):
                part = jnp.zeros((tq, LANES), F32)
                for j in range(n):
                    ge = (key_scr[j] >= cand).astype(F32)
                    for u in range(tq // LANES):
                        part = part + ge[:, u * LANES:(u + 1) * LANES]
                return jnp.sum(part, axis=1, keepdims=True)

            thr_scr[...] = _kth_largest_key(count_ge, (tq, 1), k_top)

    thr = thr_scr[...]

    row = lax.broadcasted_iota(I32, (tq, tq), 0)
    col = lax.broadcasted_iota(I32, (tq, tq), 1)
    causal = col <= row
    mrun_scr[...] = jnp.full(mrun_scr.shape, NEG, F32)

    def score_chunk(j, bias_of, extra):
        off = pl.multiple_of(j * tq, tq)
        sel = key_scr[j] >= thr
        if extra is not None:
            sel = sel & extra
        for hh in range(B_HEADS):
            kv_cols = slice((hh // B_GROUP) * B_HD, (hh // B_GROUP + 1) * B_HD)
            qh = qb_ref[0, :, hh * B_HD:(hh + 1) * B_HD]
            s = _dot_nt(qh, kbb_ref[0, pl.ds(off, tq), kv_cols]) * B_SCALE + bias_of(hh)
            s = jnp.where(sel, s, NEG)
            s_scr[hh, j] = s
            mrun_scr[hh] = jnp.maximum(mrun_scr[hh], _fold_lanes(s, jnp.maximum))

    def far(j, carry):
        score_chunk(j, lambda hh: cfar_ref[hh], None)
        return carry

    lax.fori_loop(0, jnp.maximum(i - 1, 0), far, 0)

    @pl.when(i > 0)
    def _():
        score_chunk(i - 1, lambda hh: bias_ref[hh, :, :tq], None)

    score_chunk(i, lambda hh: bias_ref[hh, :, tq:], causal)
    for hh in range(B_HEADS):
        _row_max_all_lanes(mrun_scr, hh)
    lrun_scr[...] = jnp.zeros(lrun_scr.shape, F32)
    acc_scr[...] = jnp.zeros(acc_scr.shape, F32)

    def value_chunk(j, carry):
        off = pl.multiple_of(j * tq, tq)
        for hh in range(B_HEADS):
            kv_cols = slice((hh // B_GROUP) * B_HD, (hh // B_GROUP + 1) * B_HD)
            _softmax_value_step(s_scr, mrun_scr, lrun_scr, acc_scr, hh, j, vbb_ref[0, pl.ds(off, tq), kv_cols])
        return carry

    lax.fori_loop(0, i + 1, value_chunk, 0)
    for hh in range(B_HEADS):
        cols = slice(hh * B_HD, (hh + 1) * B_HD)
        l = jnp.sum(lrun_scr[hh], axis=-1, keepdims=True)
        out_ref[0, :, cols] = ((acc_scr[hh] / l) * _silu(gb_ref[0, :, cols])).astype(BF16)


def _dsa_prompt(cfar, qi2, wi, ii, qb, kbb, vbb, gate_b, bias, tq, k_top):
    nb, seq = ii.shape[0], ii.shape[1]
    cw = min(512, seq)
    assert cw % tq == 0 and seq % cw == 0 and k_top <= tq
    full = lambda w: pl.BlockSpec((1, seq, w), lambda b, i: (b, 0, 0))
    tile = lambda w: pl.BlockSpec((1, tq, w), lambda b, i: (b, i, 0))
    return pl.pallas_call(
        functools.partial(_dsa_prompt_kernel, tq=tq, seq=seq, k_top=k_top, cw=cw),
        grid=(nb, seq // tq),
        in_specs=[pl.BlockSpec(memory_space=pltpu.SMEM),
                  pl.BlockSpec((IDX_HEADS, 1, tq, LANES), lambda b, i: (0, b, i, 0)),
                  tile(LANES), full(LANES), tile(B_WIDTH), full(B_KVW), full(B_KVW), tile(B_WIDTH),
                  _resident(bias.shape)],
        out_specs=tile(B_WIDTH),
        out_shape=jax.ShapeDtypeStruct((nb, seq, B_WIDTH), BF16),
        scratch_shapes=[pltpu.VMEM((seq // tq, tq, tq), I32), pltpu.VMEM((tq, 1), I32),
                        pltpu.VMEM((B_HEADS, seq // tq, tq, tq), F32),
                        pltpu.VMEM((B_HEADS, tq, LANES), F32), pltpu.VMEM((B_HEADS, tq, LANES), F32),
                        pltpu.VMEM((B_HEADS, tq, B_HD), F32)],
        compiler_params=_params(("parallel", "parallel")),
    )(cfar, qi2, wi, ii, qb, kbb, vbb, gate_b, bias)


def _bias_kernel(table_ref, bucket_ref, out_ref):
    bucket = bucket_ref[...]
    for hh in range(B_HEADS):
        acc = jnp.zeros(bucket.shape, F32)
        for b in range(REL_BUCKETS):
            acc = jnp.where(bucket == b, table_ref[b, hh], acc)
        out_ref[hh] = acc


def _t5_bucket_np(rel):
    n = np.maximum(rel, 0)
    exact = REL_BUCKETS // 2
    nf = np.maximum(n, 1).astype(np.float32)
    large = exact + (np.log(nf / exact) / math.log(REL_MAX_DIST / exact)
                     * (REL_BUCKETS - exact)).astype(np.int32)
    return np.where(n < exact, n, np.minimum(large, REL_BUCKETS - 1)).astype(np.int32)


def _bias_tiles(rel_table, rel):
    bucket = jnp.asarray(_t5_bucket_np(rel))
    return pl.pallas_call(
        _bias_kernel,
        in_specs=[pl.BlockSpec(memory_space=pltpu.SMEM), pl.BlockSpec(memory_space=pltpu.VMEM)],
        out_specs=pl.BlockSpec(memory_space=pltpu.VMEM),
        out_shape=jax.ShapeDtypeStruct((B_HEADS,) + rel.shape, F32),
    )(rel_table, bucket)


def _out_proj_kernel(x_ref, ma_ref, mb_ref, p_ref, wout_ref, gple_ref, wgate_ref, wproj_ref, gfin_ref,
                     y_ref, x1_scr, hn_scr, *, cw):
    d = x_ref.shape[1]
    nc = d // cw
    half = ma_ref.shape[1]
    ss = jnp.zeros((x_ref.shape[0], 1), F32)
    for c in range(nc):
        cols = slice(c * cw, (c + 1) * cw)
        x1 = (x_ref[:, cols] + _dot(ma_ref[...], wout_ref[:half, cols])
              + _dot(mb_ref[...], wout_ref[half:, cols]))
        x1_scr[:, cols] = x1
        ss = ss + jnp.sum(x1 * x1, axis=-1, keepdims=True)
    rs = lax.rsqrt(ss / d + EPS)
    for c in range(nc):
        cols = slice(c * cw, (c + 1) * cw)
        hn_scr[:, cols] = (x1_scr[:, cols] * rs * gple_ref[:, cols]).astype(BF16)
    pb = p_ref[...].astype(BF16)
    ss = jnp.zeros((x_ref.shape[0], 1), F32)
    for c in range(nc):
        cols = slice(c * cw, (c + 1) * cw)
        gt = 1.0 / (1.0 + jnp.exp(-_dot(hn_scr[...], wgate_ref[:, cols])))
        x2 = x1_scr[:, cols] + gt * _dot(pb, wproj_ref[:, cols])
        x1_scr[:, cols] = x2
        ss = ss + jnp.sum(x2 * x2, axis=-1, keepdims=True)
    rs = lax.rsqrt(ss / d + EPS)
    for c in range(nc):
        cols = slice(c * cw, (c + 1) * cw)
        y_ref[:, cols] = x1_scr[:, cols] * rs * gfin_ref[:, cols]


def _out_proj(x, ma, mb, p, w_out, g_ple, w_gate, w_proj, g_final, tm):
    n, d = x.shape
    row = lambda w: pl.BlockSpec((tm, w), lambda i: (i, 0))
    return pl.pallas_call(
        functools.partial(_out_proj_kernel, cw=min(512, d)),
        grid=(n // tm,),
        in_specs=[row(d), row(ma.shape[1]), row(mb.shape[1]), row(p.shape[1]),
                  _resident(w_out.shape), _resident(g_ple.shape), _resident(w_gate.shape),
                  _resident(w_proj.shape), _resident(g_final.shape)],
        out_specs=row(d),
        out_shape=jax.ShapeDtypeStruct((n, d), F32),
        scratch_shapes=[pltpu.VMEM((tm, d), F32), pltpu.VMEM((tm, d), BF16)],
        compiler_params=_params(("parallel",)),
    )(x, ma, mb, p, w_out, g_ple, w_gate, w_proj, g_final)


def _page_copies(pt_ref, pools, bufs, lane_major, sems, q, chunk, slot, pages_per_chunk):
    copies = []
    for k, (pool, buf, lm) in enumerate(zip(pools, bufs, lane_major)):
        rows, width = pool.shape[2], pool.shape[3]
        for p in range(pages_per_chunk):
            pid = pt_ref[q, chunk * pages_per_chunk + p]
            if lm:
                dst = buf.at[slot, :, pl.ds(p * width, width)]
            else:
                dst = buf.at[slot, pl.ds(p * rows, rows)]
            copies.append(pltpu.make_async_copy(pool.at[0, pid], dst, sems.at[slot, k]))
    return copies


def _sample_pass1_kernel(pt_ref, qlat_ref, qpe_ref, qidx_ref, wi_ref, ckvn_ref, kpen_ref, kidxn_ref,
                         ga_ref, wuv_ref, pool_ckv, pool_kpe, pool_kidx,
                         out_ref, score_ref,
                         buf_ckv, buf_kpe, buf_kidx, sems, m_scr, l_scr, acc_scr,
                         *, n_chunks, pages_per_chunk, page, past):
    g = pl.program_id(0)
    n_steps = pl.num_programs(0)
    q = g // n_chunks
    c = g % n_chunks
    slot = g % 2
    pools = (pool_ckv, pool_kpe, pool_kidx)
    bufs = (buf_ckv, buf_kpe, buf_kidx)
    ck = pages_per_chunk * page

    def copies(step, slot_):
        return _page_copies(pt_ref, pools, bufs, (False, True, True), sems, step // n_chunks,
                            step % n_chunks, slot_, pages_per_chunk)

    def wait_slot(slot_):
        for k, buf in enumerate(bufs):
            pltpu.make_async_copy(buf.at[slot_], buf.at[slot_], sems.at[slot_, k]).wait()

    @pl.when(g == 0)
    def _():
        for cp in copies(g, slot):
            cp.start()

    @pl.when(g + 1 < n_steps)
    def _():
        for cp in copies(g + 1, 1 - slot):
            cp.start()

    qlat = qlat_ref[0].astype(F32)
    qpe = qpe_ref[0].astype(F32)
    qidx = qidx_ref[0].astype(F32)
    wcol = wi_ref[0]

    @pl.when(c == 0)
    def _():
        ckvn = ckvn_ref[0]
        s_new = (jnp.sum(qlat * ckvn, axis=-1, keepdims=True)
                 + jnp.sum(qpe * kpen_ref[0], axis=-1, keepdims=True)) * MLA_SCALE
        m_scr[...] = s_new
        l_scr[...] = jnp.ones_like(s_new)
        acc_scr[...] = jnp.broadcast_to(ckvn, acc_scr.shape)
        si = jnp.maximum(jnp.sum(qidx * kidxn_ref[0], axis=-1, keepdims=True), 0.0) * wcol
        si = jnp.sum(si, axis=0, keepdims=True)
        lane = lax.broadcasted_iota(I32, (1, SCORE_PAD), 1)
        score_ref[0, :, past:] = jnp.where(lane == 0, si, -jnp.inf)

    wait_slot(slot)

    kc = buf_ckv[slot].astype(BF16)
    s = (_dot_nt(qlat.astype(BF16), kc) + _dot(qpe, buf_kpe[slot])) * MLA_SCALE
    m, l, acc = _flash_update((m_scr[...], l_scr[...], acc_scr[...]), s, None, kc)
    m_scr[...] = m
    l_scr[...] = l
    acc_scr[...] = acc

    si = jnp.maximum(_dot(qidx, buf_kidx[slot]), 0.0) * wcol
    score_ref[0, :, pl.ds(pl.multiple_of(c * ck, ck), ck)] = jnp.sum(si, axis=0, keepdims=True)

    @pl.when(c == n_chunks - 1)
    def _():
        lat = (acc / l).astype(BF16)
        for hh in range(A_HEADS):
            o = _dot(lat, wuv_ref[hh])[hh:hh + 1]
            cols = slice(hh * A_V, (hh + 1) * A_V)
            out_ref[0, :, cols] = o * _silu(ga_ref[0, :, cols])


def _sample_pass1(page_table, qlat, qpe, qidx, wi, ckvn, kpen, kidxn, gate_a, w_uv,
                  pool_ckv, pool_kpe, pool_kidx, pages_per_chunk):
    nq, n_pages = page_table.shape
    page = pool_ckv.shape[2]
    past = n_pages * page
    n_chunks = n_pages // pages_per_chunk
    ck = pages_per_chunk * page
    per_q = lambda a: pl.BlockSpec((1,) + a.shape[1:], lambda g, pt: (g // n_chunks, 0, 0))
    any_spec = pl.BlockSpec(memory_space=pl.ANY)
    grid_spec = pltpu.PrefetchScalarGridSpec(
        num_scalar_prefetch=1,
        grid=(nq * n_chunks,),
        in_specs=[per_q(qlat), per_q(qpe), per_q(qidx), per_q(wi), per_q(ckvn), per_q(kpen),
                  per_q(kidxn), per_q(gate_a),
                  pl.BlockSpec(w_uv.shape, lambda g, pt: (0, 0, 0), pipeline_mode=pl.Buffered(1)),
                  any_spec, any_spec, any_spec],
        out_specs=[pl.BlockSpec((1, 1, A_WIDTH), lambda g, pt: (g // n_chunks, 0, 0)),
                   pl.BlockSpec((1, 1, past + SCORE_PAD), lambda g, pt: (g // n_chunks, 0, 0))],
        scratch_shapes=[pltpu.VMEM((2, ck, KV_LORA), F32), pltpu.VMEM((2, A_ROPE, ck), F32),
                        pltpu.VMEM((2, IDX_DIM, ck), F32), pltpu.SemaphoreType.DMA((2, 3)),
                        pltpu.VMEM((A_HEADS, 1), F32), pltpu.VMEM((A_HEADS, 1), F32),
                        pltpu.VMEM((A_HEADS, KV_LORA), F32)],
    )
    return pl.pallas_call(
        functools.partial(_sample_pass1_kernel, n_chunks=n_chunks, pages_per_chunk=pages_per_chunk,
                          page=page, past=past),
        grid_spec=grid_spec,
        out_shape=[jax.ShapeDtypeStruct((nq, 1, A_WIDTH), F32),
                   jax.ShapeDtypeStruct((nq, 1, past + SCORE_PAD), F32)],
        compiler_params=_params(("arbitrary",)),
    )(page_table, qlat, qpe, qidx, wi, ckvn, kpen, kidxn, gate_a, w_uv, pool_ckv, pool_kpe, pool_kidx)


def _topk_thresh_kernel(score_ref, thr_ref, *, k_top):
    key = _sort_key(score_ref[...])

    def count_ge(cand):
        ge = (key >= cand).astype(F32)
        return jnp.sum(jnp.sum(ge, axis=1, keepdims=True), axis=2, keepdims=True)

    thr = _kth_largest_key(count_ge, (key.shape[0], 1, 1), k_top)
    thr_ref[...] = jnp.broadcast_to(thr, thr_ref.shape)


def _topk_thresh(score3, k_top):
    nq = score3.shape[0]
    return pl.pallas_call(
        functools.partial(_topk_thresh_kernel, k_top=k_top),
        in_specs=[pl.BlockSpec(memory_space=pltpu.VMEM)],
        out_specs=pl.BlockSpec(memory_space=pltpu.VMEM),
        out_shape=jax.ShapeDtypeStruct((nq, 1, LANES), I32),
        compiler_params=_params(),
    )(score3)


ROW_TILE = 8
SLOTS_PER_TILE = ROW_TILE // B_KV_HEADS


def _select_rows_kernel(score_ref, thr_ref, pt_ref, tix_ref, sub_ref, nsel_ref, *, n_pages, page, n_rows):
    key = _sort_key(score_ref[0, :n_pages, :])
    prow = lax.broadcasted_iota(I32, (n_pages, page), 0)
    m = ((key >= thr_ref[0][:, :1]) & (prow < n_pages - 1)).astype(F32)
    r_i = lax.broadcasted_iota(I32, (n_pages, n_pages), 0)
    c_i = lax.broadcasted_iota(I32, (n_pages, n_pages), 1)
    cnt = _dot(m.astype(BF16), jnp.ones((page, LANES), BF16))
    o_excl = _dot((c_i < r_i).astype(BF16), cnt.astype(BF16))
    o_incl = o_excl + cnt
    reps = n_rows // LANES
    o_ex = jnp.concatenate([o_excl] * reps, axis=1)
    o_in = jnp.concatenate([o_incl] * reps, axis=1)
    jrow = lax.broadcasted_iota(I32, (n_pages, n_rows), 1).astype(F32)
    hit = ((o_ex <= jrow) & (jrow < o_in)).astype(F32)
    pt_j = jnp.sum(hit * pt_ref[0], axis=0, keepdims=True)
    o_j = jnp.sum(hit * o_ex, axis=0, keepdims=True)
    used = jnp.sum(hit, axis=0, keepdims=True)
    rank = jrow[:1] - o_j
    s_r = lax.broadcasted_iota(I32, (page, page), 0)
    s_c = lax.broadcasted_iota(I32, (page, page), 1)
    pst = _dot((s_c <= s_r).astype(BF16), m.T.astype(BF16))
    psj = _dot(pst.astype(BF16), hit.astype(BF16))
    slot_j = jnp.sum((psj <= rank).astype(F32), axis=0, keepdims=True)
    quad = jnp.floor(slot_j * (1.0 / SLOTS_PER_TILE))
    tile = pt_j * (page // SLOTS_PER_TILE) + quad
    tix_ref[0] = jnp.where(used > 0.0, tile, 0.0).astype(I32)
    sub_ref[0] = ((slot_j - SLOTS_PER_TILE * quad) * B_KV_HEADS).astype(I32)
    nsel_ref[0] = o_incl[n_pages - 1:n_pages, :].astype(I32)


def _select_rows(score3, thr, pt_col, n_rows):
    nq, rows, page = score3.shape
    n_pages = pt_col.shape[1]
    assert n_rows % LANES == 0 and page == LANES and n_pages % 8 == 0
    per_q = lambda a: pl.BlockSpec((1,) + a.shape[1:], lambda q: (q, 0, 0))
    out = lambda w: pl.BlockSpec((1, 1, w), lambda q: (q, 0, 0))
    return pl.pallas_call(
        functools.partial(_select_rows_kernel, n_pages=n_pages, page=page, n_rows=n_rows),
        grid=(nq,),
        in_specs=[per_q(score3), per_q(thr), per_q(pt_col)],
        out_specs=[out(n_rows), out(n_rows), out(LANES)],
        out_shape=[jax.ShapeDtypeStruct((nq, 1, n_rows), I32), jax.ShapeDtypeStruct((nq, 1, n_rows), I32),
                   jax.ShapeDtypeStruct((nq, 1, LANES), I32)],
        compiler_params=_params(("parallel",)),
    )(score3, thr, pt_col)


def _sample_pass2_kernel(pt_ref, tix_ref, cfar_ref, qb_ref, kbn_ref, vbn_ref, gb_ref, sub_ref, nsel_ref,
                         slast_ref, snew_ref, thr_ref, tail_ref, bnew_ref, pool_k, pool_v, out_ref,
                         buf_k, buf_v, sems, *, n_pages, page, n_rows):
    q = pl.program_id(0)
    nq = pl.num_programs(0)
    slot = q % 2
    gw = ROW_TILE * n_rows
    lw = B_KV_HEADS * page
    pools_bufs = ((pool_k, buf_k), (pool_v, buf_v))

    def start_all(qq, slot_):
        for j in range(n_rows):
            off = pl.multiple_of(tix_ref[qq, j] * ROW_TILE, ROW_TILE)
            for k, (pool, buf) in enumerate(pools_bufs):
                pltpu.make_async_copy(pool.at[pl.ds(off, ROW_TILE)],
                                      buf.at[slot_, pl.ds(j * ROW_TILE, ROW_TILE)],
                                      sems.at[slot_, k]).start(priority=k)
        last = pl.multiple_of(pt_ref[qq, n_pages - 1] * lw, lw)
        for k, (pool, buf) in enumerate(pools_bufs):
            pltpu.make_async_copy(pool.at[pl.ds(last, lw)], buf.at[slot_, pl.ds(gw, lw)],
                                  sems.at[slot_, k]).start()

    @pl.when(q == 0)
    def _():
        start_all(q, slot)

    @pl.when(q + 1 < nq)
    def _():
        start_all(q + 1, 1 - slot)

    qb = qb_ref[0].astype(F32)
    thr = thr_ref[0][:, :1]
    kv_of_row = (lax.broadcasted_iota(I32, (B_HEADS, 1), 0) >= B_GROUP).astype(I32)
    head0 = kv_of_row == 0
    lo, hi = slice(0, B_HD), slice(B_HD, 2 * B_HD)

    keep = _sort_key(snew_ref[0, :, :LANES])[:, :1] >= thr
    kbn = kbn_ref[0]
    vbn = vbn_ref[0]
    s_new = jnp.where(head0, jnp.sum(qb * kbn[:, lo], axis=-1, keepdims=True),
                      jnp.sum(qb * kbn[:, hi], axis=-1, keepdims=True))
    s_new = jnp.where(keep, s_new * B_SCALE + bnew_ref[...], NEG)
    v_new = jnp.where(head0, jnp.broadcast_to(vbn[:, lo], (B_HEADS, B_HD)),
                      jnp.broadcast_to(vbn[:, hi], (B_HEADS, B_HD)))

    lane_g = lax.broadcasted_iota(I32, (B_HEADS, gw), 1)
    valid_g = ((lane_g & (ROW_TILE - 1)) == sub_ref[0] + kv_of_row) & (lane_g < ROW_TILE * nsel_ref[0][:, :1])
    lane_l = lax.broadcasted_iota(I32, (B_HEADS, lw), 1)
    sel_l = _sort_key(slast_ref[0]) >= thr
    valid_l = jnp.broadcast_to(sel_l, (B_HEADS, lw)) & ((lane_l & 1) == kv_of_row)
    valid = jnp.concatenate([valid_g, valid_l], axis=1)
    bias = jnp.concatenate([jnp.broadcast_to(cfar_ref[...], (B_HEADS, gw)), tail_ref[...]], axis=1)

    for k, (pool, buf) in enumerate(pools_bufs):
        pltpu.make_async_copy(pool.at[pl.ds(0, gw + lw)], buf.at[slot], sems.at[slot, k]).wait()

    s = _dot_nt(qb, buf_k[slot]) * B_SCALE + bias
    s = jnp.where(valid, s, NEG)
    m = jnp.maximum(jnp.max(s, axis=-1, keepdims=True), s_new)
    p = jnp.where(valid, jnp.exp(s - m), 0.0)
    p_new = jnp.where(keep, jnp.exp(s_new - m), 0.0)
    l = jnp.sum(p, axis=-1, keepdims=True) + p_new
    o = (_dot(p, buf_v[slot]) + p_new * v_new) / l
    for hh in range(B_HEADS):
        cols = slice(hh * B_HD, (hh + 1) * B_HD)
        out_ref[0, :, cols] = o[hh:hh + 1] * _silu(gb_ref[0, :, cols])


def _sample_pass2(page_table, tix, cfar, qb, kbn, vbn, gate_b, sub_rep, nsel, slast2, score, thr, tail2,
                  bias_new, pool_k2, pool_v2, page):
    nq, n_pages = page_table.shape
    n_rows = tix.shape[1]
    past = n_pages * page
    rows = ROW_TILE * n_rows + B_KV_HEADS * page
    assert past % SCORE_PAD == 0
    per_q = lambda a: pl.BlockSpec((1,) + a.shape[1:], lambda q, pt, tx: (q, 0, 0))
    const2 = lambda a: pl.BlockSpec(a.shape, lambda q, pt, tx: (0, 0))
    any_spec = pl.BlockSpec(memory_space=pl.ANY)
    grid_spec = pltpu.PrefetchScalarGridSpec(
        num_scalar_prefetch=2,
        grid=(nq,),
        in_specs=[const2(cfar), per_q(qb), per_q(kbn), per_q(vbn), per_q(gate_b), per_q(sub_rep),
                  per_q(nsel), per_q(slast2),
                  pl.BlockSpec((1, 1, SCORE_PAD), lambda q, pt, tx: (q, 0, past // SCORE_PAD)),
                  per_q(thr), const2(tail2), const2(bias_new), any_spec, any_spec],
        out_specs=pl.BlockSpec((1, 1, B_WIDTH), lambda q, pt, tx: (q, 0, 0)),
        scratch_shapes=[pltpu.VMEM((2, rows, B_HD), F32), pltpu.VMEM((2, rows, B_HD), F32),
                        pltpu.SemaphoreType.DMA((2, 2))],
    )
    return pl.pallas_call(
        functools.partial(_sample_pass2_kernel, n_pages=n_pages, page=page, n_rows=n_rows),
        grid_spec=grid_spec,
        out_shape=jax.ShapeDtypeStruct((nq, 1, B_WIDTH), F32),
        compiler_params=_params(("arbitrary",)),
    )(page_table, tix, cfar, qb, kbn, vbn, gate_b, sub_rep, nsel, slast2, score, thr, tail2, bias_new,
      pool_k2, pool_v2)


def _prep_w_in(w_in):
    offs = np.concatenate([[0], np.cumsum(IN_SIZES)])
    part = lambda k: w_in[:, offs[k]:offs[k + 1]]
    c_q, c_kv, k_pe, gate_a, q_b, k_b, v_b, gate_b, q_idx, k_idx, w_idx = [part(k) for k in range(11)]
    pad = jnp.zeros((w_in.shape[0], LANES - IDX_HEADS), w_in.dtype)
    cols = [c_q, c_kv, gate_a, q_b, k_b, v_b, gate_b, q_idx, k_pe, k_idx, w_idx, pad]
    return jnp.concatenate(cols, axis=1).astype(BF16)


def _prep_w_uq(w_uq):
    w = w_uq.reshape(w_uq.shape[0], A_HEADS, A_NOPE + A_ROPE)
    nope = w[:, :, :A_NOPE].reshape(w_uq.shape[0], A_HEADS * A_NOPE)
    rope = w[:, :, A_NOPE:].reshape(w_uq.shape[0], A_HEADS * A_ROPE)
    return jnp.concatenate([nope, rope], axis=1).astype(BF16)


def kernel(x_prompt, x_sample, p_prompt, p_sample, cache_ckv, cache_kpe, cache_k, cache_v, cache_kidx,
           page_table, rel_table, g_attn, w_in, g_q, w_uq, g_kv, w_uk, w_uv, w_out, g_ple, w_ple_gate,
           w_ple_proj, g_final):
    nb, seq, d = x_prompt.shape
    nq, nt, _ = x_sample.shape
    depth = w_in.shape[0]
    assert depth == 1 and nt == 1
    n_pages = page_table.shape[1]
    page = cache_ckv.shape[2]
    past = n_pages * page
    tq = min(256, seq)
    tm_s = min(128, nq)
    pages_per_chunk = min(64, n_pages)
    assert seq % tq == 0 and nq % tm_s == 0 and n_pages % pages_per_chunk == 0
    assert tq + 1 > REL_MAX_DIST

    row2 = lambda v: v.reshape(1, -1)
    w_in_p = _prep_w_in(w_in[0])
    w_uq_p = _prep_w_uq(w_uq[0])
    w_uk_b = w_uk[0].astype(BF16)
    w_uv_b = w_uv[0].astype(BF16)
    w_out_b = w_out[0].astype(BF16)
    w_gate_b = w_ple_gate[0].astype(BF16)
    w_proj_b = w_ple_proj[0].astype(BF16)
    cfar = rel_table[REL_BUCKETS - 1]

    tabs_p = _rope_tables(jnp.arange(seq, dtype=I32))
    xp = x_prompt.reshape(nb * seq, d)
    (qcat, qi2, _, _, wi, qb, ga, gb, ckv, kpe, kb, vb, kidx, kcat, ii, kbb, vbb) = _proj_in(
        xp, tabs_p, seq // tq, row2(g_attn[0]), w_in_p, row2(g_q[0]), w_uq_p, row2(g_kv[0]), w_uk_b, tq)
    b3 = lambda a: a.reshape((nb, seq) + a.shape[1:])
    b4 = lambda a: a.reshape((a.shape[0], nb, seq) + a.shape[2:])
    ma = _mla_prompt(b4(qcat), b3(kcat), b3(ga), w_uv_b, tq)
    rel = np.arange(tq)[:, None] + tq - np.arange(2 * tq)[None, :]
    bias = _bias_tiles(rel_table, rel)
    k_top = min(IDX_TOPK, seq // 4)
    mb = _dsa_prompt(cfar, b4(qi2), b3(wi), b3(ii), b3(qb), b3(kbb), b3(vbb), b3(gb), bias, tq, k_top)
    y_prompt = _out_proj(xp, ma.reshape(nb * seq, -1), mb.reshape(nb * seq, -1),
                         p_prompt[0].reshape(nb * seq, -1), w_out_b, row2(g_ple[0]), w_gate_b, w_proj_b,
                         row2(g_final), tq).reshape(nb, seq, d)

    tabs_s = _rope_tables(jnp.full((tm_s,), past, I32))
    xs = x_sample.reshape(nq, d)
    (qcat_s, _, qpe_s, qidx_s, wi_s, qb_s, ga_s, gb_s, ckv_s, kpe_s, kb_s, vb_s, kidx_s, _, _, _, _) = _proj_in(
        xs, tabs_s, 1, row2(g_attn[0]), w_in_p, row2(g_q[0]), w_uq_p, row2(g_kv[0]), w_uk_b, tm_s)
    qlat_s = jnp.swapaxes(qcat_s[:, :, :KV_LORA], 0, 1)
    ma_s, score = _sample_pass1(
        page_table, qlat_s, qpe_s.reshape(nq, A_HEADS, A_ROPE), qidx_s.reshape(nq, IDX_HEADS, IDX_DIM),
        wi_s[:, :IDX_HEADS].reshape(nq, IDX_HEADS, 1), ckv_s.reshape(nq, 1, -1), kpe_s.reshape(nq, 1, -1),
        kidx_s.reshape(nq, 1, -1), ga_s.reshape(nq, 1, -1), w_uv_b, cache_ckv,
        jnp.swapaxes(cache_kpe, 2, 3), jnp.swapaxes(cache_kidx, 2, 3), pages_per_chunk)
    k_top_s = min(IDX_TOPK, (past + nt) // 4)
    score3 = score.reshape(nq, n_pages + SCORE_PAD // LANES, LANES)
    thr = _topk_thresh(score3, k_top_s)
    tix, sub, nsel = _select_rows(score3, thr, page_table.astype(F32).reshape(nq, n_pages, 1), k_top_s)
    rel_tail = np.broadcast_to(np.concatenate(
        [np.repeat(page - np.arange(page), 2), np.zeros(LANES, np.int64)])[None, :], (8, 2 * page + LANES))
    tail = _bias_tiles(rel_table, rel_tail)[:, 0, :]
    rows2 = lambda pool: pool.reshape(pool.shape[1] * B_KV_HEADS * page, B_HD)
    mb_s = _sample_pass2(page_table, tix.reshape(nq, -1), cfar.reshape(B_HEADS, 1),
                         qb_s.reshape(nq, B_HEADS, B_HD), kb_s.reshape(nq, 1, -1), vb_s.reshape(nq, 1, -1),
                         gb_s.reshape(nq, 1, -1), jnp.repeat(sub, ROW_TILE, axis=2), nsel,
                         jnp.repeat(score[:, :, past - page:past], 2, axis=2), score, thr,
                         tail[:, :2 * page], tail[:, 2 * page:2 * page + 1],
                         rows2(cache_k), rows2(cache_v), page)
    y_sample = _out_proj(xs, ma_s.reshape(nq, -1).astype(BF16), mb_s.reshape(nq, -1).astype(BF16),
                         p_sample[0].reshape(nq, -1), w_out_b, row2(g_ple[0]), w_gate_b, w_proj_b,
                         row2(g_final), tm_s).reshape(nq, nt, d)

    return (y_prompt, y_sample,
            ckv.reshape(1, nb, seq, -1), kpe.reshape(1, nb, seq, -1),
            kb.reshape(1, nb, seq, B_KV_HEADS, B_HD), vb.reshape(1, nb, seq, B_KV_HEADS, B_HD),
            kidx.reshape(1, nb, seq, -1),
            ckv_s.reshape(1, nq, nt, -1), kpe_s.reshape(1, nq, nt, -1),
            kb_s.reshape(1, nq, nt, B_KV_HEADS, B_HD), vb_s.reshape(1, nq, nt, B_KV_HEADS, B_HD),
            kidx_s.reshape(1, nq, nt, -1))
```

```python
import functools
import math

import numpy as np
import jax
import jax.numpy as jnp
from jax import lax
from jax.experimental import pallas as pl
from jax.experimental.pallas import tpu as pltpu

F32 = jnp.float32
BF16 = jnp.bfloat16
I32 = jnp.int32

A_HEADS = 8
A_NOPE = 128
A_ROPE = 64
A_V = 128
A_WIDTH = A_HEADS * A_V
Q_LORA = 512
KV_LORA = 256
B_HEADS = 8
B_KV_HEADS = 2
B_HD = 128
B_GROUP = B_HEADS // B_KV_HEADS
B_WIDTH = B_HEADS * B_HD
B_KVW = B_KV_HEADS * B_HD
IDX_HEADS = 16
IDX_DIM = 64
IDX_ROPE = 32
IDX_TOPK = 256
REL_BUCKETS = 32
REL_MAX_DIST = 128
ROPE_THETA = 10000.0
EPS = 1e-6
MLA_SCALE = (A_NOPE + A_ROPE) ** -0.5
B_SCALE = B_HD ** -0.5
IDX_W_SCALE = (IDX_HEADS ** -0.5) * (IDX_DIM ** -0.5)
IN_SIZES = (Q_LORA, KV_LORA, A_ROPE, A_WIDTH, B_WIDTH, B_KVW, B_KVW, B_WIDTH,
            IDX_HEADS * IDX_DIM, IDX_DIM, IDX_HEADS)

LANES = 128
NEG = -1e30
VMEM_LIMIT = 56 * 1024 * 1024
INT_MIN = -2 ** 31

C_Q = 0
C_KV = C_Q + Q_LORA
C_GA = C_KV + KV_LORA
C_QB = C_GA + A_WIDTH
C_KB = C_QB + B_WIDTH
C_VB = C_KB + B_KVW
C_GB = C_VB + B_KVW
C_QI = C_GB + B_WIDTH
C_SLAB = C_QI + IDX_HEADS * IDX_DIM
C_WI = C_SLAB + LANES
NP_IN = C_WI + LANES
QCAT = KV_LORA + LANES
N_TAB = 11
SCORE_PAD = 8 * LANES


def _dot(a, b):
    return jnp.dot(a, b, preferred_element_type=F32)


def _dot_nt(a, b):
    return lax.dot_general(a, b, (((1,), (1,)), ((), ())), preferred_element_type=F32)


def _silu(x):
    return x / (1.0 + jnp.exp(-x))


def _params(sem=None):
    return pltpu.CompilerParams(dimension_semantics=sem, vmem_limit_bytes=VMEM_LIMIT)


def _resident(shape):
    nd = len(shape)
    return pl.BlockSpec(shape, lambda *_: (0,) * nd, pipeline_mode=pl.Buffered(1))


def _rope_tables(pos):
    def cs(half):
        inv = ROPE_THETA ** (-jnp.arange(half, dtype=F32) / half)
        ang = pos.astype(F32)[:, None] * inv[None, :]
        return jnp.cos(ang), jnp.sin(ang)
    n = pos.shape[0]
    c32, s32 = cs(A_ROPE // 2)
    c16, s16 = cs(IDX_ROPE // 2)
    z16 = jnp.zeros((n, 16), F32)
    z32 = jnp.zeros((n, 32), F32)
    z64 = jnp.zeros((n, 64), F32)
    o32 = jnp.ones((n, 32), F32)
    cat = lambda *a: jnp.concatenate(a, axis=1)
    a64, bm64, bp64 = cat(c32, c32), cat(-s32, z32), cat(z32, s32)
    a16, cm16, cp16 = cat(c16, c16, o32), cat(-s16, z16, z32), cat(z16, s16, z32)
    tabs = [cat(a64, a64), cat(bm64, bm64), cat(bp64, bp64),
            cat(a16, a16), cat(cm16, cm16), cat(cp16, cp16),
            cat(a64, a16), cat(bm64, z64), cat(bp64, z64), cat(z64, cm16), cat(z64, cp16)]
    return jnp.concatenate(tabs, axis=1)


def _proj_in_kernel(x_ref, gattn_ref, win_ref, gq_ref, wuq_ref, gkv_ref, wuk_ref, tab_ref,
                    qcat_ref, qi2_ref, qpe_ref, qidx_ref, wi_ref, qb_ref, ga_ref, gb_ref,
                    ckv_ref, kpe_ref, kb_ref, vb_ref, kidx_ref, kcat_ref, ii_ref, kbb_ref, vbb_ref,
                    h_scr):
    x = x_ref[...]
    ms = jnp.mean(x * x, axis=-1, keepdims=True)
    h_scr[...] = (x * lax.rsqrt(ms + EPS) * gattn_ref[...]).astype(BF16)

    def zcols(c0, width):
        return _dot(h_scr[...], win_ref[:, c0:c0 + width])

    def tab(k):
        return tab_ref[:, k * LANES:(k + 1) * LANES]

    tm = x.shape[0]
    lane = lax.broadcasted_iota(I32, (tm, LANES), 1)
    lo = lane < 64

    def rope(xg, a, bm, bp, shift):
        return (xg * tab(a) + pltpu.roll(xg, LANES - shift, 1) * tab(bm)
                + pltpu.roll(xg, shift, 1) * tab(bp))

    cq = zcols(C_Q, Q_LORA)
    cq = cq * lax.rsqrt(jnp.mean(cq * cq, axis=-1, keepdims=True) + EPS) * gq_ref[...]
    q = _dot(cq.astype(BF16), wuq_ref[...])
    n_nope = A_HEADS * A_NOPE
    for g in range(A_HEADS // 2):
        xg = q[:, n_nope + g * LANES:n_nope + (g + 1) * LANES]
        rg = rope(xg, 0, 1, 2, A_ROPE // 2)
        qpe_ref[:, g * LANES:(g + 1) * LANES] = rg.astype(BF16)
        qcat_ref[2 * g, :, KV_LORA:] = jnp.where(lo, rg, 0.0).astype(BF16)
        qcat_ref[2 * g + 1, :, KV_LORA:] = jnp.where(lo, 0.0, rg).astype(BF16)
    for hh in range(A_HEADS):
        qn = q[:, hh * A_NOPE:(hh + 1) * A_NOPE].astype(BF16)
        qcat_ref[hh, :, :KV_LORA] = _dot(qn, wuk_ref[hh]).astype(BF16)

    ckv = zcols(C_KV, KV_LORA)
    ckv = ckv * lax.rsqrt(jnp.mean(ckv * ckv, axis=-1, keepdims=True) + EPS) * gkv_ref[...]
    ckv_ref[...] = ckv
    kcat_ref[:, :KV_LORA] = ckv.astype(BF16)

    slab = zcols(C_SLAB, LANES)
    slab = (slab * tab(6) + pltpu.roll(slab, LANES - 32, 1) * tab(7) + pltpu.roll(slab, 32, 1) * tab(8)
            + pltpu.roll(slab, LANES - 16, 1) * tab(9) + pltpu.roll(slab, 16, 1) * tab(10))
    swapped = pltpu.roll(slab, 64, 1)
    kpe_ref[...] = slab[:, :A_ROPE]
    kidx_ref[...] = swapped[:, :IDX_DIM]
    kcat_ref[:, KV_LORA:] = jnp.where(lo, slab, swapped).astype(BF16)
    ii_ref[...] = jnp.where(lo, swapped, slab).astype(BF16)

    kb = zcols(C_KB, B_KVW)
    kb_ref[...] = kb
    kbb_ref[...] = kb.astype(BF16)
    vb = zcols(C_VB, B_KVW)
    vb_ref[...] = vb
    vbb_ref[...] = vb.astype(BF16)
    for c in range(B_WIDTH // 512):
        qb_ref[:, c * 512:(c + 1) * 512] = zcols(C_QB + c * 512, 512).astype(BF16)
        ga_ref[:, c * 512:(c + 1) * 512] = zcols(C_GA + c * 512, 512)
        gb_ref[:, c * 512:(c + 1) * 512] = zcols(C_GB + c * 512, 512)

    for c in range(IDX_HEADS * IDX_DIM // 512):
        qi = zcols(C_QI + c * 512, 512)
        for g in range(4):
            rg = rope(qi[:, g * LANES:(g + 1) * LANES], 3, 4, 5, IDX_ROPE // 2)
            col = c * 512 + g * LANES
            qidx_ref[:, col:col + LANES] = rg.astype(BF16)
            hh = col // IDX_DIM
            qi2_ref[hh] = jnp.where(lo, rg, 0.0).astype(BF16)
            qi2_ref[hh + 1] = jnp.where(lo, 0.0, rg).astype(BF16)
    wi_ref[...] = zcols(C_WI, LANES) * IDX_W_SCALE


def _proj_in(x, tabs, tab_blocks, g_attn, w_in_p, g_q, w_uq_p, g_kv, w_uk, tm):
    n, d = x.shape
    grid = (n // tm,)
    row = lambda w: pl.BlockSpec((tm, w), lambda i: (i, 0))
    head = lambda nh, w: pl.BlockSpec((nh, tm, w), lambda i: (0, i, 0))
    out_shape = [
        jax.ShapeDtypeStruct((A_HEADS, n, QCAT), BF16),
        jax.ShapeDtypeStruct((IDX_HEADS, n, LANES), BF16),
        jax.ShapeDtypeStruct((n, A_HEADS * A_ROPE), BF16),
        jax.ShapeDtypeStruct((n, IDX_HEADS * IDX_DIM), BF16),
        jax.ShapeDtypeStruct((n, LANES), F32),
        jax.ShapeDtypeStruct((n, B_WIDTH), BF16),
        jax.ShapeDtypeStruct((n, A_WIDTH), F32),
        jax.ShapeDtypeStruct((n, B_WIDTH), F32),
        jax.ShapeDtypeStruct((n, KV_LORA), F32),
        jax.ShapeDtypeStruct((n, A_ROPE), F32),
        jax.ShapeDtypeStruct((n, B_KVW), F32),
        jax.ShapeDtypeStruct((n, B_KVW), F32),
        jax.ShapeDtypeStruct((n, IDX_DIM), F32),
        jax.ShapeDtypeStruct((n, QCAT), BF16),
        jax.ShapeDtypeStruct((n, LANES), BF16),
        jax.ShapeDtypeStruct((n, B_KVW), BF16),
        jax.ShapeDtypeStruct((n, B_KVW), BF16),
    ]
    out_specs = [head(A_HEADS, QCAT), head(IDX_HEADS, LANES), row(A_HEADS * A_ROPE),
                 row(IDX_HEADS * IDX_DIM), row(LANES), row(B_WIDTH), row(A_WIDTH), row(B_WIDTH),
                 row(KV_LORA), row(A_ROPE), row(B_KVW), row(B_KVW), row(IDX_DIM),
                 row(QCAT), row(LANES), row(B_KVW), row(B_KVW)]
    in_specs = [row(d), _resident(g_attn.shape), _resident(w_in_p.shape), _resident(g_q.shape),
                _resident(w_uq_p.shape), _resident(g_kv.shape), _resident(w_uk.shape),
                pl.BlockSpec((tm, N_TAB * LANES), lambda i: (i % tab_blocks, 0))]
    return pl.pallas_call(
        _proj_in_kernel, grid=grid, in_specs=in_specs, out_specs=out_specs, out_shape=out_shape,
        scratch_shapes=[pltpu.VMEM((tm, d), BF16)],
        compiler_params=_params(("parallel",)),
    )(x, g_attn, w_in_p, g_q, w_uq_p, g_kv, w_uk, tabs)


def _flash_update(carry, s, valid, v):
    m, l, acc = carry
    if valid is not None:
        s = jnp.where(valid, s, NEG)
    m_new = jnp.maximum(m, jnp.max(s, axis=-1, keepdims=True))
    p = jnp.exp(s - m_new)
    if valid is not None:
        p = jnp.where(valid, p, 0.0)
    alpha = jnp.exp(m - m_new)
    l = alpha * l + jnp.sum(p, axis=-1, keepdims=True)
    acc = alpha * acc + _dot(p.astype(v.dtype), v)
    return m_new, l, acc


def _fold_lanes(x, op):
    out = x[:, :LANES]
    for u in range(1, x.shape[1] // LANES):
        out = op(out, x[:, u * LANES:(u + 1) * LANES])
    return out


def _row_max_all_lanes(mrun_scr, hh):
    m = jnp.max(mrun_scr[hh], axis=-1, keepdims=True)
    mrun_scr[hh] = jnp.broadcast_to(m, mrun_scr.shape[1:])


def _softmax_value_step(s_scr, mrun_scr, lrun_scr, acc_scr, hh, j, v):
    m = mrun_scr[hh]
    p = jnp.exp(s_scr[hh, j] - jnp.concatenate([m] * (s_scr.shape[3] // LANES), axis=1))
    lrun_scr[hh] = lrun_scr[hh] + _fold_lanes(p, jnp.add)
    acc_scr[hh] = acc_scr[hh] + _dot(p.astype(v.dtype), v)


def _mla_prompt_kernel(qcat_ref, kcat_ref, ga_ref, wuv_ref, out_ref, s_scr, mrun_scr, lrun_scr, acc_scr,
                       *, tq):
    i = pl.program_id(1)
    row = lax.broadcasted_iota(I32, (tq, tq), 0)
    col = lax.broadcasted_iota(I32, (tq, tq), 1)
    causal = col <= row
    mrun_scr[...] = jnp.full(mrun_scr.shape, NEG, F32)

    def score_chunk(j, valid):
        kc = kcat_ref[0, pl.ds(pl.multiple_of(j * tq, tq), tq), :]
        for hh in range(A_HEADS):
            s = _dot_nt(qcat_ref[hh, 0], kc) * MLA_SCALE
            if valid is not None:
                s = jnp.where(valid, s, NEG)
            s_scr[hh, j] = s
            mrun_scr[hh] = jnp.maximum(mrun_scr[hh], _fold_lanes(s, jnp.maximum))

    def far(j, carry):
        score_chunk(j, None)
        return carry

    lax.fori_loop(0, i, far, 0)
    score_chunk(i, causal)
    for hh in range(A_HEADS):
        _row_max_all_lanes(mrun_scr, hh)
    lrun_scr[...] = jnp.zeros(lrun_scr.shape, F32)
    acc_scr[...] = jnp.zeros(acc_scr.shape, F32)

    def value_chunk(j, carry):
        v = kcat_ref[0, pl.ds(pl.multiple_of(j * tq, tq), tq), :KV_LORA]
        for hh in range(A_HEADS):
            _softmax_value_step(s_scr, mrun_scr, lrun_scr, acc_scr, hh, j, v)
        return carry

    lax.fori_loop(0, i + 1, value_chunk, 0)
    for hh in range(A_HEADS):
        l = jnp.sum(lrun_scr[hh], axis=-1, keepdims=True)
        lat = (acc_scr[hh] / l).astype(BF16)
        o = _dot(lat, wuv_ref[hh])
        cols = slice(hh * A_V, (hh + 1) * A_V)
        out_ref[0, :, cols] = (o * _silu(ga_ref[0, :, cols])).astype(BF16)


def _mla_prompt(qcat, kcat, gate_a, w_uv, tq):
    nb, seq = kcat.shape[0], kcat.shape[1]
    return pl.pallas_call(
        functools.partial(_mla_prompt_kernel, tq=tq),
        grid=(nb, seq // tq),
        in_specs=[pl.BlockSpec((A_HEADS, 1, tq, QCAT), lambda b, i: (0, b, i, 0)),
                  pl.BlockSpec((1, seq, QCAT), lambda b, i: (b, 0, 0)),
                  pl.BlockSpec((1, tq, A_WIDTH), lambda b, i: (b, i, 0)),
                  _resident(w_uv.shape)],
        out_specs=pl.BlockSpec((1, tq, A_WIDTH), lambda b, i: (b, i, 0)),
        out_shape=jax.ShapeDtypeStruct((nb, seq, A_WIDTH), BF16),
        scratch_shapes=[pltpu.VMEM((A_HEADS, seq // tq, tq, tq), F32),
                        pltpu.VMEM((A_HEADS, tq, LANES), F32), pltpu.VMEM((A_HEADS, tq, LANES), F32),
                        pltpu.VMEM((A_HEADS, tq, KV_LORA), F32)],
        compiler_params=_params(("parallel", "parallel")),
    )(qcat, kcat, gate_a, w_uv)


def _sort_key(score):
    b = pltpu.bitcast(score + 0.0, I32)
    return jnp.where(b < 0, b ^ 0x7FFFFFFF, b)


def _kth_largest_key(count_ge, shape, k):
    def body(it, lo):
        bit = 31 - it
        cand = lo + lax.shift_left(jnp.int32(1), bit)
        return jnp.where(count_ge(cand) >= k, cand, lo)
    return lax.fori_loop(0, 32, body, jnp.full(shape, INT_MIN, I32), unroll=8)


def _dsa_prompt_kernel(cfar_ref, qi2_ref, wi_ref, ii_ref, qb_ref, kbb_ref, vbb_ref, gb_ref, bias_ref,
                       out_ref, key_scr, thr_scr, s_scr, mrun_scr, lrun_scr, acc_scr, *, tq, seq, k_top, cw):
    i = pl.program_id(1)
    t0 = i * tq
    w = wi_ref[0]
    qrow = t0 + lax.broadcasted_iota(I32, (tq, cw), 0)
    kcol = lax.broadcasted_iota(I32, (tq, cw), 1)

    for c in range(seq // cw):
        @pl.when(c * cw < t0 + tq)
        def _():
            kc = ii_ref[0, c * cw:(c + 1) * cw, :]
            acc = jnp.zeros((tq, cw), F32)
            for hh in range(IDX_HEADS):
                s = _dot_nt(qi2_ref[hh, 0], kc)
                acc = acc + w[:, hh:hh + 1] * jnp.maximum(s, 0.0)
            acc = jnp.where(kcol + c * cw <= qrow, acc, -jnp.inf)
            key = _sort_key(acc)
            for u in range(cw // tq):
                key_scr[c * (cw // tq) + u] = key[:, u * tq:(u + 1) * tq]

    for n in range(1, seq // tq + 1):
        @pl.when(i == n - 1)
        def _():
            def count_ge(cand):
                part = jnp.zeros((tq, LANES), F32)
                for j in range(n):
                    ge = (key_scr[j] >= cand).astype(F32)
                    for u in range(tq // LANES):
                        part = part + ge[:, u * LANES:(u + 1) * LANES]
                return jnp.sum(part, axis=1, keepdims=True)

            thr_scr[...] = _kth_largest_key(count_ge, (tq, 1), k_top)

    thr = thr_scr[...]

    row = lax.broadcasted_iota(I32, (tq, tq), 0)
    col = lax.broadcasted_iota(I32, (tq, tq), 1)
    causal = col <= row
    mrun_scr[...] = jnp.full(mrun_scr.shape, NEG, F32)

    def score_chunk(j, bias_of, extra):
        off = pl.multiple_of(j * tq, tq)
        sel = key_scr[j] >= thr
        if extra is not None:
            sel = sel & extra
        for hh in range(B_HEADS):
            kv_cols = slice((hh // B_GROUP) * B_HD, (hh // B_GROUP + 1) * B_HD)
            qh = qb_ref[0, :, hh * B_HD:(hh + 1) * B_HD]
            s = _dot_nt(qh, kbb_ref[0, pl.ds(off, tq), kv_cols]) * B_SCALE + bias_of(hh)
            s = jnp.where(sel, s, NEG)
            s_scr[hh, j] = s
            mrun_scr[hh] = jnp.maximum(mrun_scr[hh], _fold_lanes(s, jnp.maximum))

    def far(j, carry):
        score_chunk(j, lambda hh: cfar_ref[hh], None)
        return carry

    lax.fori_loop(0, jnp.maximum(i - 1, 0), far, 0)

    @pl.when(i > 0)
    def _():
        score_chunk(i - 1, lambda hh: bias_ref[hh, :, :tq], None)

    score_chunk(i, lambda hh: bias_ref[hh, :, tq:], causal)
    for hh in range(B_HEADS):
        _row_max_all_lanes(mrun_scr, hh)
    lrun_scr[...] = jnp.zeros(lrun_scr.shape, F32)
    acc_scr[...] = jnp.zeros(acc_scr.shape, F32)

    def value_chunk(j, carry):
        off = pl.multiple_of(j * tq, tq)
        for hh in range(B_HEADS):
            kv_cols = slice((hh // B_GROUP) * B_HD, (hh // B_GROUP + 1) * B_HD)
            _softmax_value_step(s_scr, mrun_scr, lrun_scr, acc_scr, hh, j, vbb_ref[0, pl.ds(off, tq), kv_cols])
        return carry

    lax.fori_loop(0, i + 1, value_chunk, 0)
    for hh in range(B_HEADS):
        cols = slice(hh * B_HD, (hh + 1) * B_HD)
        l = jnp.sum(lrun_scr[hh], axis=-1, keepdims=True)
        out_ref[0, :, cols] = ((acc_scr[hh] / l) * _silu(gb_ref[0, :, cols])).astype(BF16)


def _dsa_prompt(cfar, qi2, wi, ii, qb, kbb, vbb, gate_b, bias, tq, k_top):
    nb, seq = ii.shape[0], ii.shape[1]
    cw = min(512, seq)
    assert cw % tq == 0 and seq % cw == 0 and k_top <= tq
    full = lambda w: pl.BlockSpec((1, seq, w), lambda b, i: (b, 0, 0))
    tile = lambda w: pl.BlockSpec((1, tq, w), lambda b, i: (b, i, 0))
    return pl.pallas_call(
        functools.partial(_dsa_prompt_kernel, tq=tq, seq=seq, k_top=k_top, cw=cw),
        grid=(nb, seq // tq),
        in_specs=[pl.BlockSpec(memory_space=pltpu.SMEM),
                  pl.BlockSpec((IDX_HEADS, 1, tq, LANES), lambda b, i: (0, b, i, 0)),
                  tile(LANES), full(LANES), tile(B_WIDTH), full(B_KVW), full(B_KVW), tile(B_WIDTH),
                  _resident(bias.shape)],
        out_specs=tile(B_WIDTH),
        out_shape=jax.ShapeDtypeStruct((nb, seq, B_WIDTH), BF16),
        scratch_shapes=[pltpu.VMEM((seq // tq, tq, tq), I32), pltpu.VMEM((tq, 1), I32),
                        pltpu.VMEM((B_HEADS, seq // tq, tq, tq), F32),
                        pltpu.VMEM((B_HEADS, tq, LANES), F32), pltpu.VMEM((B_HEADS, tq, LANES), F32),
                        pltpu.VMEM((B_HEADS, tq, B_HD), F32)],
        compiler_params=_params(("parallel", "parallel")),
    )(cfar, qi2, wi, ii, qb, kbb, vbb, gate_b, bias)


def _bias_kernel(table_ref, bucket_ref, out_ref):
    bucket = bucket_ref[...]
    for hh in range(B_HEADS):
        acc = jnp.zeros(bucket.shape, F32)
        for b in range(REL_BUCKETS):
            acc = jnp.where(bucket == b, table_ref[b, hh], acc)
        out_ref[hh] = acc


def _t5_bucket_np(rel):
    n = np.maximum(rel, 0)
    exact = REL_BUCKETS // 2
    nf = np.maximum(n, 1).astype(np.float32)
    large = exact + (np.log(nf / exact) / math.log(REL_MAX_DIST / exact)
                     * (REL_BUCKETS - exact)).astype(np.int32)
    return np.where(n < exact, n, np.minimum(large, REL_BUCKETS - 1)).astype(np.int32)


def _bias_tiles(rel_table, rel):
    bucket = jnp.asarray(_t5_bucket_np(rel))
    return pl.pallas_call(
        _bias_kernel,
        in_specs=[pl.BlockSpec(memory_space=pltpu.SMEM), pl.BlockSpec(memory_space=pltpu.VMEM)],
        out_specs=pl.BlockSpec(memory_space=pltpu.VMEM),
        out_shape=jax.ShapeDtypeStruct((B_HEADS,) + rel.shape, F32),
    )(rel_table, bucket)


def _out_proj_kernel(x_ref, ma_ref, mb_ref, p_ref, wout_ref, gple_ref, wgate_ref, wproj_ref, gfin_ref,
                     y_ref, x1_scr, hn_scr, *, cw):
    d = x_ref.shape[1]
    nc = d // cw
    half = ma_ref.shape[1]
    ss = jnp.zeros((x_ref.shape[0], 1), F32)
    for c in range(nc):
        cols = slice(c * cw, (c + 1) * cw)
        x1 = (x_ref[:, cols] + _dot(ma_ref[...], wout_ref[:half, cols])
              + _dot(mb_ref[...], wout_ref[half:, cols]))
        x1_scr[:, cols] = x1
        ss = ss + jnp.sum(x1 * x1, axis=-1, keepdims=True)
    rs = lax.rsqrt(ss / d + EPS)
    for c in range(nc):
        cols = slice(c * cw, (c + 1) * cw)
        hn_scr[:, cols] = (x1_scr[:, cols] * rs * gple_ref[:, cols]).astype(BF16)
    pb = p_ref[...].astype(BF16)
    ss = jnp.zeros((x_ref.shape[0], 1), F32)
    for c in range(nc):
        cols = slice(c * cw, (c + 1) * cw)
        gt = 1.0 / (1.0 + jnp.exp(-_dot(hn_scr[...], wgate_ref[:, cols])))
        x2 = x1_scr[:, cols] + gt * _dot(pb, wproj_ref[:, cols])
        x1_scr[:, cols] = x2
        ss = ss + jnp.sum(x2 * x2, axis=-1, keepdims=True)
    rs = lax.rsqrt(ss / d + EPS)
    for c in range(nc):
        cols = slice(c * cw, (c + 1) * cw)
        y_ref[:, cols] = x1_scr[:, cols] * rs * gfin_ref[:, cols]


def _out_proj(x, ma, mb, p, w_out, g_ple, w_gate, w_proj, g_final, tm):
    n, d = x.shape
    row = lambda w: pl.BlockSpec((tm, w), lambda i: (i, 0))
    return pl.pallas_call(
        functools.partial(_out_proj_kernel, cw=min(512, d)),
        grid=(n // tm,),
        in_specs=[row(d), row(ma.shape[1]), row(mb.shape[1]), row(p.shape[1]),
                  _resident(w_out.shape), _resident(g_ple.shape), _resident(w_gate.shape),
                  _resident(w_proj.shape), _resident(g_final.shape)],
        out_specs=row(d),
        out_shape=jax.ShapeDtypeStruct((n, d), F32),
        scratch_shapes=[pltpu.VMEM((tm, d), F32), pltpu.VMEM((tm, d), BF16)],
        compiler_params=_params(("parallel",)),
    )(x, ma, mb, p, w_out, g_ple, w_gate, w_proj, g_final)


def _page_copies(pt_ref, pools, bufs, lane_major, sems, q, chunk, slot, pages_per_chunk):
    copies = []
    for k, (pool, buf, lm) in enumerate(zip(pools, bufs, lane_major)):
        rows, width = pool.shape[2], pool.shape[3]
        for p in range(pages_per_chunk):
            pid = pt_ref[q, chunk * pages_per_chunk + p]
            if lm:
                dst = buf.at[slot, :, pl.ds(p * width, width)]
            else:
                dst = buf.at[slot, pl.ds(p * rows, rows)]
            copies.append(pltpu.make_async_copy(pool.at[0, pid], dst, sems.at[slot, k]))
    return copies


def _sample_pass1_kernel(pt_ref, qlat_ref, qpe_ref, qidx_ref, wi_ref, ckvn_ref, kpen_ref, kidxn_ref,
                         ga_ref, wuv_ref, pool_ckv, pool_kpe, pool_kidx,
                         out_ref, score_ref,
                         buf_ckv, buf_kpe, buf_kidx, sems, m_scr, l_scr, acc_scr,
                         *, n_chunks, pages_per_chunk, page, past):
    g = pl.program_id(0)
    n_steps = pl.num_programs(0)
    q = g // n_chunks
    c = g % n_chunks
    slot = g % 2
    pools = (pool_ckv, pool_kpe, pool_kidx)
    bufs = (buf_ckv, buf_kpe, buf_kidx)
    ck = pages_per_chunk * page

    def copies(step, slot_):
        return _page_copies(pt_ref, pools, bufs, (False, True, True), sems, step // n_chunks,
                            step % n_chunks, slot_, pages_per_chunk)

    def wait_slot(slot_):
        for k, buf in enumerate(bufs):
            pltpu.make_async_copy(buf.at[slot_], buf.at[slot_], sems.at[slot_, k]).wait()

    @pl.when(g == 0)
    def _():
        for cp in copies(g, slot):
            cp.start()

    @pl.when(g + 1 < n_steps)
    def _():
        for cp in copies(g + 1, 1 - slot):
            cp.start()

    qlat = qlat_ref[0].astype(F32)
    qpe = qpe_ref[0].astype(F32)
    qidx = qidx_ref[0].astype(F32)
    wcol = wi_ref[0]

    @pl.when(c == 0)
    def _():
        ckvn = ckvn_ref[0]
        s_new = (jnp.sum(qlat * ckvn, axis=-1, keepdims=True)
                 + jnp.sum(qpe * kpen_ref[0], axis=-1, keepdims=True)) * MLA_SCALE
        m_scr[...] = s_new
        l_scr[...] = jnp.ones_like(s_new)
        acc_scr[...] = jnp.broadcast_to(ckvn, acc_scr.shape)
        si = jnp.maximum(jnp.sum(qidx * kidxn_ref[0], axis=-1, keepdims=True), 0.0) * wcol
        si = jnp.sum(si, axis=0, keepdims=True)
        lane = lax.broadcasted_iota(I32, (1, SCORE_PAD), 1)
        score_ref[0, :, past:] = jnp.where(lane == 0, si, -jnp.inf)

    wait_slot(slot)

    kc = buf_ckv[slot].astype(BF16)
    s = (_dot_nt(qlat.astype(BF16), kc) + _dot(qpe, buf_kpe[slot])) * MLA_SCALE
    m, l, acc = _flash_update((m_scr[...], l_scr[...], acc_scr[...]), s, None, kc)
    m_scr[...] = m
    l_scr[...] = l
    acc_scr[...] = acc

    si = jnp.maximum(_dot(qidx, buf_kidx[slot]), 0.0) * wcol
    score_ref[0, :, pl.ds(pl.multiple_of(c * ck, ck), ck)] = jnp.sum(si, axis=0, keepdims=True)

    @pl.when(c == n_chunks - 1)
    def _():
        lat = (acc / l).astype(BF16)
        for hh in range(A_HEADS):
            o = _dot(lat, wuv_ref[hh])[hh:hh + 1]
            cols = slice(hh * A_V, (hh + 1) * A_V)
            out_ref[0, :, cols] = o * _silu(ga_ref[0, :, cols])


def _sample_pass1(page_table, qlat, qpe, qidx, wi, ckvn, kpen, kidxn, gate_a, w_uv,
                  pool_ckv, pool_kpe, pool_kidx, pages_per_chunk):
    nq, n_pages = page_table.shape
    page = pool_ckv.shape[2]
    past = n_pages * page
    n_chunks = n_pages // pages_per_chunk
    ck = pages_per_chunk * page
    per_q = lambda a: pl.BlockSpec((1,) + a.shape[1:], lambda g, pt: (g // n_chunks, 0, 0))
    any_spec = pl.BlockSpec(memory_space=pl.ANY)
    grid_spec = pltpu.PrefetchScalarGridSpec(
        num_scalar_prefetch=1,
        grid=(nq * n_chunks,),
        in_specs=[per_q(qlat), per_q(qpe), per_q(qidx), per_q(wi), per_q(ckvn), per_q(kpen),
                  per_q(kidxn), per_q(gate_a),
                  pl.BlockSpec(w_uv.shape, lambda g, pt: (0, 0, 0), pipeline_mode=pl.Buffered(1)),
                  any_spec, any_spec, any_spec],
        out_specs=[pl.BlockSpec((1, 1, A_WIDTH), lambda g, pt: (g // n_chunks, 0, 0)),
                   pl.BlockSpec((1, 1, past + SCORE_PAD), lambda g, pt: (g // n_chunks, 0, 0))],
        scratch_shapes=[pltpu.VMEM((2, ck, KV_LORA), F32), pltpu.VMEM((2, A_ROPE, ck), F32),
                        pltpu.VMEM((2, IDX_DIM, ck), F32), pltpu.SemaphoreType.DMA((2, 3)),
                        pltpu.VMEM((A_HEADS, 1), F32), pltpu.VMEM((A_HEADS, 1), F32),
                        pltpu.VMEM((A_HEADS, KV_LORA), F32)],
    )
    return pl.pallas_call(
        functools.partial(_sample_pass1_kernel, n_chunks=n_chunks, pages_per_chunk=pages_per_chunk,
                          page=page, past=past),
        grid_spec=grid_spec,
        out_shape=[jax.ShapeDtypeStruct((nq, 1, A_WIDTH), F32),
                   jax.ShapeDtypeStruct((nq, 1, past + SCORE_PAD), F32)],
        compiler_params=_params(("arbitrary",)),
    )(page_table, qlat, qpe, qidx, wi, ckvn, kpen, kidxn, gate_a, w_uv, pool_ckv, pool_kpe, pool_kidx)


def _topk_thresh_kernel(score_ref, thr_ref, *, k_top):
    key = _sort_key(score_ref[...])

    def count_ge(cand):
        ge = (key >= cand).astype(F32)
        return jnp.sum(jnp.sum(ge, axis=1, keepdims=True), axis=2, keepdims=True)

    thr = _kth_largest_key(count_ge, (key.shape[0], 1, 1), k_top)
    thr_ref[...] = jnp.broadcast_to(thr, thr_ref.shape)


def _topk_thresh(score3, k_top):
    nq = score3.shape[0]
    return pl.pallas_call(
        functools.partial(_topk_thresh_kernel, k_top=k_top),
        in_specs=[pl.BlockSpec(memory_space=pltpu.VMEM)],
        out_specs=pl.BlockSpec(memory_space=pltpu.VMEM),
        out_shape=jax.ShapeDtypeStruct((nq, 1, LANES), I32),
        compiler_params=_params(),
    )(score3)


ROW_TILE = 8
SLOTS_PER_TILE = ROW_TILE // B_KV_HEADS


def _select_rows_kernel(score_ref, thr_ref, pt_ref, tix_ref, sub_ref, nsel_ref, *, n_pages, page, n_rows):
    prow = lax.broadcasted_iota(I32, (n_pages, page), 0)
    r_i = lax.broadcasted_iota(I32, (n_pages, n_pages), 0)
    c_i = lax.broadcasted_iota(I32, (n_pages, n_pages), 1)
    earlier = (c_i < r_i).astype(BF16)
    s_r = lax.broadcasted_iota(I32, (page, page), 0)
    s_c = lax.broadcasted_iota(I32, (page, page), 1)
    upto = (s_c <= s_r).astype(BF16)
    ones = jnp.ones((page, LANES), BF16)
    reps = n_rows // LANES
    jrow = lax.broadcasted_iota(I32, (n_pages, n_rows), 1).astype(F32)
    for b in range(score_ref.shape[0]):
        key = _sort_key(score_ref[b, :n_pages, :])
        m = ((key >= thr_ref[b][:, :1]) & (prow < n_pages - 1)).astype(F32)
        cnt = _dot(m.astype(BF16), ones)
        o_excl = _dot(earlier, cnt.astype(BF16))
        o_incl = o_excl + cnt
        o_ex = jnp.concatenate([o_excl] * reps, axis=1)
        o_in = jnp.concatenate([o_incl] * reps, axis=1)
        hit = ((o_ex <= jrow) & (jrow < o_in)).astype(F32)
        pt_j = jnp.sum(hit * pt_ref[b], axis=0, keepdims=True)
        o_j = jnp.sum(hit * o_ex, axis=0, keepdims=True)
        used = jnp.sum(hit, axis=0, keepdims=True)
        rank = jrow[:1] - o_j
        pst = _dot(upto, m.T.astype(BF16))
        psj = _dot(pst.astype(BF16), hit.astype(BF16))
        slot_j = jnp.sum((psj <= rank).astype(F32), axis=0, keepdims=True)
        quad = jnp.floor(slot_j * (1.0 / SLOTS_PER_TILE))
        tile = pt_j * (page // SLOTS_PER_TILE) + quad
        tix_ref[b] = jnp.where(used > 0.0, tile, 0.0).astype(I32)
        sub_ref[b] = ((slot_j - SLOTS_PER_TILE * quad) * B_KV_HEADS).astype(I32)
        nsel_ref[b] = o_incl[n_pages - 1:n_pages, :].astype(I32)


def _select_rows(score3, thr, pt_col, n_rows):
    nq, rows, page = score3.shape
    n_pages = pt_col.shape[1]
    qb = min(8, nq)
    assert n_rows % LANES == 0 and page == LANES and n_pages % 8 == 0 and nq % qb == 0
    per_q = lambda a: pl.BlockSpec((qb,) + a.shape[1:], lambda q: (q, 0, 0))
    out = lambda w: pl.BlockSpec((qb, 1, w), lambda q: (q, 0, 0))
    return pl.pallas_call(
        functools.partial(_select_rows_kernel, n_pages=n_pages, page=page, n_rows=n_rows),
        grid=(nq // qb,),
        in_specs=[per_q(score3), per_q(thr), per_q(pt_col)],
        out_specs=[out(n_rows), out(n_rows), out(LANES)],
        out_shape=[jax.ShapeDtypeStruct((nq, 1, n_rows), I32), jax.ShapeDtypeStruct((nq, 1, n_rows), I32),
                   jax.ShapeDtypeStruct((nq, 1, LANES), I32)],
        compiler_params=_params(("parallel",)),
    )(score3, thr, pt_col)


def _sample_pass2_kernel(pt_ref, tix_ref, cfar_ref, qb_ref, kbn_ref, vbn_ref, gb_ref, sub_ref, nsel_ref,
                         slast_ref, snew_ref, thr_ref, tail_ref, bnew_ref, pool_k, pool_v, out_ref,
                         buf_k, buf_v, sems, *, n_pages, page, n_rows):
    q = pl.program_id(0)
    nq = pl.num_programs(0)
    slot = q % 2
    gw = ROW_TILE * n_rows
    lw = B_KV_HEADS * page
    pools_bufs = ((pool_k, buf_k), (pool_v, buf_v))

    def start_all(qq, slot_):
        for j in range(n_rows):
            off = pl.multiple_of(tix_ref[qq, j] * ROW_TILE, ROW_TILE)
            for k, (pool, buf) in enumerate(pools_bufs):
                pltpu.make_async_copy(pool.at[pl.ds(off, ROW_TILE)],
                                      buf.at[slot_, pl.ds(j * ROW_TILE, ROW_TILE)],
                                      sems.at[slot_, k]).start(priority=k)
        last = pl.multiple_of(pt_ref[qq, n_pages - 1] * lw, lw)
        for k, (pool, buf) in enumerate(pools_bufs):
            pltpu.make_async_copy(pool.at[pl.ds(last, lw)], buf.at[slot_, pl.ds(gw, lw)],
                                  sems.at[slot_, k]).start()

    @pl.when(q == 0)
    def _():
        start_all(q, slot)

    @pl.when(q + 1 < nq)
    def _():
        start_all(q + 1, 1 - slot)

    qb = qb_ref[0].astype(F32)
    thr = thr_ref[0][:, :1]
    kv_of_row = (lax.broadcasted_iota(I32, (B_HEADS, 1), 0) >= B_GROUP).astype(I32)
    head0 = kv_of_row == 0
    lo, hi = slice(0, B_HD), slice(B_HD, 2 * B_HD)

    keep = _sort_key(snew_ref[0, :, :LANES])[:, :1] >= thr
    kbn = kbn_ref[0]
    vbn = vbn_ref[0]
    s_new = jnp.where(head0, jnp.sum(qb * kbn[:, lo], axis=-1, keepdims=True),
                      jnp.sum(qb * kbn[:, hi], axis=-1, keepdims=True))
    s_new = jnp.where(keep, s_new * B_SCALE + bnew_ref[...], NEG)
    v_new = jnp.where(head0, jnp.broadcast_to(vbn[:, lo], (B_HEADS, B_HD)),
                      jnp.broadcast_to(vbn[:, hi], (B_HEADS, B_HD)))

    lane_g = lax.broadcasted_iota(I32, (B_HEADS, gw), 1)
    valid_g = ((lane_g & (ROW_TILE - 1)) == sub_ref[0] + kv_of_row) & (lane_g < ROW_TILE * nsel_ref[0][:, :1])
    lane_l = lax.broadcasted_iota(I32, (B_HEADS, lw), 1)
    sel_l = _sort_key(slast_ref[0]) >= thr
    valid_l = jnp.broadcast_to(sel_l, (B_HEADS, lw)) & ((lane_l & 1) == kv_of_row)
    valid = jnp.concatenate([valid_g, valid_l], axis=1)
    bias = jnp.concatenate([jnp.broadcast_to(cfar_ref[...], (B_HEADS, gw)), tail_ref[...]], axis=1)

    for k, (pool, buf) in enumerate(pools_bufs):
        pltpu.make_async_copy(pool.at[pl.ds(0, gw + lw)], buf.at[slot], sems.at[slot, k]).wait()

    s = _dot_nt(qb, buf_k[slot]) * B_SCALE + bias
    s = jnp.where(valid, s, NEG)
    m = jnp.maximum(jnp.max(s, axis=-1, keepdims=True), s_new)
    p = jnp.where(valid, jnp.exp(s - m), 0.0)
    p_new = jnp.where(keep, jnp.exp(s_new - m), 0.0)
    l = jnp.sum(p, axis=-1, keepdims=True) + p_new
    o = (_dot(p, buf_v[slot]) + p_new * v_new) / l
    for hh in range(B_HEADS):
        cols = slice(hh * B_HD, (hh + 1) * B_HD)
        out_ref[0, :, cols] = o[hh:hh + 1] * _silu(gb_ref[0, :, cols])


def _sample_pass2(page_table, tix, cfar, qb, kbn, vbn, gate_b, sub_rep, nsel, slast2, score, thr, tail2,
                  bias_new, pool_k2, pool_v2, page):
    nq, n_pages = page_table.shape
    n_rows = tix.shape[1]
    past = n_pages * page
    rows = ROW_TILE * n_rows + B_KV_HEADS * page
    assert past % SCORE_PAD == 0
    per_q = lambda a: pl.BlockSpec((1,) + a.shape[1:], lambda q, pt, tx: (q, 0, 0))
    const2 = lambda a: pl.BlockSpec(a.shape, lambda q, pt, tx: (0, 0))
    any_spec = pl.BlockSpec(memory_space=pl.ANY)
    grid_spec = pltpu.PrefetchScalarGridSpec(
        num_scalar_prefetch=2,
        grid=(nq,),
        in_specs=[const2(cfar), per_q(qb), per_q(kbn), per_q(vbn), per_q(gate_b), per_q(sub_rep),
                  per_q(nsel), per_q(slast2),
                  pl.BlockSpec((1, 1, SCORE_PAD), lambda q, pt, tx: (q, 0, past // SCORE_PAD)),
                  per_q(thr), const2(tail2), const2(bias_new), any_spec, any_spec],
        out_specs=pl.BlockSpec((1, 1, B_WIDTH), lambda q, pt, tx: (q, 0, 0)),
        scratch_shapes=[pltpu.VMEM((2, rows, B_HD), F32), pltpu.VMEM((2, rows, B_HD), F32),
                        pltpu.SemaphoreType.DMA((2, 2))],
    )
    return pl.pallas_call(
        functools.partial(_sample_pass2_kernel, n_pages=n_pages, page=page, n_rows=n_rows),
        grid_spec=grid_spec,
        out_shape=jax.ShapeDtypeStruct((nq, 1, B_WIDTH), F32),
        compiler_params=_params(("arbitrary",)),
    )(page_table, tix, cfar, qb, kbn, vbn, gate_b, sub_rep, nsel, slast2, score, thr, tail2, bias_new,
      pool_k2, pool_v2)


def _prep_w_in(w_in):
    offs = np.concatenate([[0], np.cumsum(IN_SIZES)])
    part = lambda k: w_in[:, offs[k]:offs[k + 1]]
    c_q, c_kv, k_pe, gate_a, q_b, k_b, v_b, gate_b, q_idx, k_idx, w_idx = [part(k) for k in range(11)]
    pad = jnp.zeros((w_in.shape[0], LANES - IDX_HEADS), w_in.dtype)
    cols = [c_q, c_kv, gate_a, q_b, k_b, v_b, gate_b, q_idx, k_pe, k_idx, w_idx, pad]
    return jnp.concatenate(cols, axis=1).astype(BF16)


def _prep_w_uq(w_uq):
    w = w_uq.reshape(w_uq.shape[0], A_HEADS, A_NOPE + A_ROPE)
    nope = w[:, :, :A_NOPE].reshape(w_uq.shape[0], A_HEADS * A_NOPE)
    rope = w[:, :, A_NOPE:].reshape(w_uq.shape[0], A_HEADS * A_ROPE)
    return jnp.concatenate([nope, rope], axis=1).astype(BF16)


def kernel(x_prompt, x_sample, p_prompt, p_sample, cache_ckv, cache_kpe, cache_k, cache_v, cache_kidx,
           page_table, rel_table, g_attn, w_in, g_q, w_uq, g_kv, w_uk, w_uv, w_out, g_ple, w_ple_gate,
           w_ple_proj, g_final):
    nb, seq, d = x_prompt.shape
    nq, nt, _ = x_sample.shape
    depth = w_in.shape[0]
    assert depth == 1 and nt == 1
    n_pages = page_table.shape[1]
    page = cache_ckv.shape[2]
    past = n_pages * page
    tq = min(256, seq)
    tm_s = min(128, nq)
    pages_per_chunk = min(64, n_pages)
    assert seq % tq == 0 and nq % tm_s == 0 and n_pages % pages_per_chunk == 0
    assert tq + 1 > REL_MAX_DIST

    row2 = lambda v: v.reshape(1, -1)
    w_in_p = _prep_w_in(w_in[0])
    w_uq_p = _prep_w_uq(w_uq[0])
    w_uk_b = w_uk[0].astype(BF16)
    w_uv_b = w_uv[0].astype(BF16)
    w_out_b = w_out[0].astype(BF16)
    w_gate_b = w_ple_gate[0].astype(BF16)
    w_proj_b = w_ple_proj[0].astype(BF16)
    cfar = rel_table[REL_BUCKETS - 1]

    tabs_p = _rope_tables(jnp.arange(seq, dtype=I32))
    xp = x_prompt.reshape(nb * seq, d)
    (qcat, qi2, _, _, wi, qb, ga, gb, ckv, kpe, kb, vb, kidx, kcat, ii, kbb, vbb) = _proj_in(
        xp, tabs_p, seq // tq, row2(g_attn[0]), w_in_p, row2(g_q[0]), w_uq_p, row2(g_kv[0]), w_uk_b, tq)
    b3 = lambda a: a.reshape((nb, seq) + a.shape[1:])
    b4 = lambda a: a.reshape((a.shape[0], nb, seq) + a.shape[2:])
    ma = _mla_prompt(b4(qcat), b3(kcat), b3(ga), w_uv_b, tq)
    rel = np.arange(tq)[:, None] + tq - np.arange(2 * tq)[None, :]
    bias = _bias_tiles(rel_table, rel)
    k_top = min(IDX_TOPK, seq // 4)
    mb = _dsa_prompt(cfar, b4(qi2), b3(wi), b3(ii), b3(qb), b3(kbb), b3(vbb), b3(gb), bias, tq, k_top)
    y_prompt = _out_proj(xp, ma.reshape(nb * seq, -1), mb.reshape(nb * seq, -1),
                         p_prompt[0].reshape(nb * seq, -1), w_out_b, row2(g_ple[0]), w_gate_b, w_proj_b,
                         row2(g_final), tq).reshape(nb, seq, d)

    tabs_s = _rope_tables(jnp.full((tm_s,), past, I32))
    xs = x_sample.reshape(nq, d)
    (qcat_s, _, qpe_s, qidx_s, wi_s, qb_s, ga_s, gb_s, ckv_s, kpe_s, kb_s, vb_s, kidx_s, _, _, _, _) = _proj_in(
        xs, tabs_s, 1, row2(g_attn[0]), w_in_p, row2(g_q[0]), w_uq_p, row2(g_kv[0]), w_uk_b, tm_s)
    qlat_s = jnp.swapaxes(qcat_s[:, :, :KV_LORA], 0, 1)
    ma_s, score = _sample_pass1(
        page_table, qlat_s, qpe_s.reshape(nq, A_HEADS, A_ROPE), qidx_s.reshape(nq, IDX_HEADS, IDX_DIM),
        wi_s[:, :IDX_HEADS].reshape(nq, IDX_HEADS, 1), ckv_s.reshape(nq, 1, -1), kpe_s.reshape(nq, 1, -1),
        kidx_s.reshape(nq, 1, -1), ga_s.reshape(nq, 1, -1), w_uv_b, cache_ckv,
        jnp.swapaxes(cache_kpe, 2, 3), jnp.swapaxes(cache_kidx, 2, 3), pages_per_chunk)
    k_top_s = min(IDX_TOPK, (past + nt) // 4)
    score3 = score.reshape(nq, n_pages + SCORE_PAD // LANES, LANES)
    thr = _topk_thresh(score3, k_top_s)
    tix, sub, nsel = _select_rows(score3, thr, page_table.astype(F32).reshape(nq, n_pages, 1), k_top_s)
    rel_tail = np.broadcast_to(np.concatenate(
        [np.repeat(page - np.arange(page), 2), np.zeros(LANES, np.int64)])[None, :], (8, 2 * page + LANES))
    tail = _bias_tiles(rel_table, rel_tail)[:, 0, :]
    rows2 = lambda pool: pool.reshape(pool.shape[1] * B_KV_HEADS * page, B_HD)
    mb_s = _sample_pass2(page_table, tix.reshape(nq, -1), cfar.reshape(B_HEADS, 1),
                         qb_s.reshape(nq, B_HEADS, B_HD), kb_s.reshape(nq, 1, -1), vb_s.reshape(nq, 1, -1),
                         gb_s.reshape(nq, 1, -1), jnp.repeat(sub, ROW_TILE, axis=2), nsel,
                         jnp.repeat(score[:, :, past - page:past], 2, axis=2), score, thr,
                         tail[:, :2 * page], tail[:, 2 * page:2 * page + 1],
                         rows2(cache_k), rows2(cache_v), page)
    y_sample = _out_proj(xs, ma_s.reshape(nq, -1).astype(BF16), mb_s.reshape(nq, -1).astype(BF16),
                         p_sample[0].reshape(nq, -1), w_out_b, row2(g_ple[0]), w_gate_b, w_proj_b,
                         row2(g_final), tm_s).reshape(nq, nt, d)

    return (y_prompt, y_sample,
            ckv.reshape(1, nb, seq, -1), kpe.reshape(1, nb, seq, -1),
            kb.reshape(1, nb, seq, B_KV_HEADS, B_HD), vb.reshape(1, nb, seq, B_KV_HEADS, B_HD),
            kidx.reshape(1, nb, seq, -1),
            ckv_s.reshape(1, nq, nt, -1), kpe_s.reshape(1, nq, nt, -1),
            kb_s.reshape(1, nq, nt, B_KV_HEADS, B_HD), vb_s.reshape(1, nq, nt, B_KV_HEADS, B_HD),
            kidx_s.reshape(1, nq, nt, -1))
```
